```python
import math
import jax, jax.numpy as jnp
from jax import lax
import numpy as np

D_MODEL = 1024
BATCH = 16
SEQ = 2048
DEPTH = 1

N_HEADS = 8
HEAD_DIM = 64
QK_WIDTH = N_HEADS * 2 * HEAD_DIM
V_HEAD_DIM = 2 * HEAD_DIM
V_WIDTH = N_HEADS * V_HEAD_DIM
ROT_DIM = HEAD_DIM // 4
ROPE_THETA = 500000.0
Q_BLOCK = 128

CONV_C = D_MODEL
CONV_W = 31

N_BRANCH = 2
D_FF = int(math.ceil(8 * D_MODEL / 3 / 256) * 256)

IN_COLS = 2 * CONV_C + 2 * QK_WIDTH + V_WIDTH + N_BRANCH * D_MODEL

kernel_name = "hybrid_conformer_diffattn_gated_block"


def rmsnorm(x, g, eps=1e-6):
    xf = x.astype(jnp.float32)
    y = xf * lax.rsqrt(jnp.mean(xf * xf, axis=-1, keepdims=True) + eps)
    return (y * g.astype(jnp.float32)).astype(x.dtype)


def layernorm(x, g, b, eps=1e-5):
    xf = x.astype(jnp.float32)
    mu = jnp.mean(xf, axis=-1, keepdims=True)
    xc = xf - mu
    var = jnp.mean(xc * xc, axis=-1, keepdims=True)
    y = xc * lax.rsqrt(var + eps) * g.astype(jnp.float32) + b.astype(jnp.float32)
    return y.astype(x.dtype)


def rope_tables(seq_len):
    inv_freq = ROPE_THETA ** (-jnp.arange(0, ROT_DIM, 2, dtype=jnp.float32) / ROT_DIM)
    pos = jnp.arange(seq_len, dtype=jnp.float32)
    ang = pos[:, None] * inv_freq[None, :]
    return jnp.cos(ang)[:, None, None, :], jnp.sin(ang)[:, None, None, :]


def apply_partial_rope(t, cos, sin):
    cos = cos.astype(t.dtype)
    sin = sin.astype(t.dtype)
    tr, tp = t[..., :ROT_DIM], t[..., ROT_DIM:]
    t1, t2 = tr[..., : ROT_DIM // 2], tr[..., ROT_DIM // 2:]
    rot = jnp.concatenate([t1 * cos - t2 * sin, t2 * cos + t1 * sin], axis=-1)
    return jnp.concatenate([rot, tp], axis=-1)


def conformer_conv_branch(u_glu, w_dw, b_dw, g_ln, b_ln, w_pw):
    a, gate = jnp.split(u_glu, 2, axis=-1)
    u = a * jax.nn.sigmoid(gate)
    u = lax.conv_general_dilated(
        u, w_dw[:, None, :].astype(u.dtype), window_strides=(1,),
        padding=[(CONV_W - 1, 0)],
        dimension_numbers=("NWC", "WIO", "NWC"),
        feature_group_count=CONV_C) + b_dw
    u = layernorm(u, g_ln, b_ln)
    u = jax.nn.silu(u)
    return u @ w_pw


def diff_attention(q, k, v, lam):
    seq_len = q.shape[1]
    scale = HEAD_DIM ** -0.5
    neg = jnp.finfo(jnp.float32).min
    outs = []
    for i in range(seq_len // Q_BLOCK):
        kv_len = (i + 1) * Q_BLOCK
        qs = q[:, i * Q_BLOCK:(i + 1) * Q_BLOCK]
        ks = k[:, :kv_len]
        vs = v[:, :kv_len]
        s = jnp.einsum("bqhcd,bkhcd->bhcqk", qs, ks).astype(jnp.float32) * scale
        qpos = i * Q_BLOCK + jnp.arange(Q_BLOCK)
        kpos = jnp.arange(kv_len)
        mask = kpos[None, :] <= qpos[:, None]
        s = jnp.where(mask, s, neg)
        p = jax.nn.softmax(s, axis=-1)
        a = p[:, :, 0] - lam * p[:, :, 1]
        outs.append(jnp.einsum("bhqk,bkhe->bqhe", a.astype(v.dtype), vs))
    return jnp.concatenate(outs, axis=1)


def setup_inputs(seed: int = 0) -> dict:
    key = jax.random.key(seed)
    ks = jax.random.split(key, 24)
    f32 = jnp.float32
    D, L = D_MODEL, DEPTH

    def nrm(k, shape, scale):
        return jax.random.normal(k, shape, f32) * scale

    def gain(k, shape):
        return 1.0 + 0.02 * jax.random.normal(k, shape, f32)

    return {
        "x": jax.random.normal(ks[0], (BATCH, SEQ, D), f32),
        "g_mix": gain(ks[1], (L, D)),
        "w_in": nrm(ks[2], (L, D, IN_COLS), D ** -0.5),
        "b_gate": nrm(ks[3], (L, N_BRANCH * D), 0.02),
        "w_dw": nrm(ks[4], (L, CONV_W, CONV_C), CONV_W ** -0.5),
        "b_dw": nrm(ks[5], (L, CONV_C), 0.02),
        "g_conv_ln": gain(ks[6], (L, CONV_C)),
        "b_conv_ln": nrm(ks[7], (L, CONV_C), 0.02),
        "w_conv_pw": nrm(ks[8], (L, CONV_C, D), CONV_C ** -0.5),
        "lambda_q1": nrm(ks[9], (L, HEAD_DIM), 0.1),
        "lambda_k1": nrm(ks[10], (L, HEAD_DIM), 0.1),
        "lambda_q2": nrm(ks[11], (L, HEAD_DIM), 0.1),
        "lambda_k2": nrm(ks[12], (L, HEAD_DIM), 0.1),
        "g_subln": gain(ks[13], (L, V_HEAD_DIM)),
        "w_attn_pw": nrm(ks[14], (L, V_WIDTH, D), V_WIDTH ** -0.5),
        "w_out": nrm(ks[15], (L, D, D), D ** -0.5),
        "g_ffn": gain(ks[16], (L, D)),
        "w_ffn_gate": nrm(ks[17], (L, D, D_FF), D ** -0.5),
        "w_ffn_up": nrm(ks[18], (L, D, D_FF), D ** -0.5),
        "w_ffn_down": nrm(ks[19], (L, D_FF, D), D_FF ** -0.5),
        "g_final": gain(ks[20], (D,)),
    }


def reference(x, g_mix, w_in, b_gate, w_dw, b_dw, g_conv_ln, b_conv_ln, w_conv_pw,
              lambda_q1, lambda_k1, lambda_q2, lambda_k2, g_subln, w_attn_pw, w_out,
              g_ffn, w_ffn_gate, w_ffn_up, w_ffn_down, g_final):
    bsz, seq_len, _ = x.shape
    cos, sin = rope_tables(seq_len)
    c0 = 2 * CONV_C
    c1 = c0 + QK_WIDTH
    c2 = c1 + QK_WIDTH
    c3 = c2 + V_WIDTH
    for l in range(DEPTH):
        lambda_init = 0.8 - 0.6 * math.exp(-0.3 * l)
        h = rmsnorm(x, g_mix[l])
        z = h @ w_in[l]
        z_conv = z[..., :c0]
        q = z[..., c0:c1].reshape(bsz, seq_len, N_HEADS, 2, HEAD_DIM)
        k = z[..., c1:c2].reshape(bsz, seq_len, N_HEADS, 2, HEAD_DIM)
        v = z[..., c2:c3].reshape(bsz, seq_len, N_HEADS, V_HEAD_DIM)
        gates = jax.nn.sigmoid(z[..., c3:] + b_gate[l])
        gate_conv, gate_attn = jnp.split(gates, N_BRANCH, axis=-1)

        y_conv = conformer_conv_branch(z_conv, w_dw[l], b_dw[l], g_conv_ln[l],
                                       b_conv_ln[l], w_conv_pw[l])

        q = apply_partial_rope(q, cos, sin)
        k = apply_partial_rope(k, cos, sin)
        lam = (jnp.exp(jnp.sum(lambda_q1[l].astype(jnp.float32) * lambda_k1[l].astype(jnp.float32)))
               - jnp.exp(jnp.sum(lambda_q2[l].astype(jnp.float32) * lambda_k2[l].astype(jnp.float32)))
               + lambda_init)
        o = diff_attention(q, k, v, lam)
        o = rmsnorm(o, g_subln[l]) * (1.0 - lambda_init)
        y_attn = o.reshape(bsz, seq_len, V_WIDTH) @ w_attn_pw[l]

        merged = gate_conv * y_conv + gate_attn * y_attn
        x = x + merged @ w_out[l]

        h2 = rmsnorm(x, g_ffn[l])
        ff = jax.nn.silu(h2 @ w_ffn_gate[l]) * (h2 @ w_ffn_up[l])
        x = x + ff @ w_ffn_down[l]
    return rmsnorm(x, g_final)
```

```python
import functools
import math

import jax
import jax.numpy as jnp
import numpy as np
from jax import lax
from jax.experimental import pallas as pl
from jax.experimental.pallas import tpu as pltpu

F32 = jnp.float32
BF16 = jnp.bfloat16

N_HEADS = 8
HEAD_DIM = 64
V_HEAD_DIM = 2 * HEAD_DIM
ROT_DIM = HEAD_DIM // 4
ROPE_THETA = 500000.0
CONV_W = 31
LANES = 128
MXU_COLS = 256
VMEM_LIMIT_BYTES = 56 * 1024 * 1024

ROW_TILE = 512
COL_CHUNK = 512
ATT_TQ = 256
ATT_TK = 256
CONV_TS = 256
CONV_HALO = 32
CONV_RC = 32
MASK_VALUE = -1e30


def _const_spec(shape):
    nd = len(shape)
    return pl.BlockSpec(shape, lambda *_: (0,) * nd, pipeline_mode=pl.Buffered(1))


def _params():
    return pltpu.CompilerParams(vmem_limit_bytes=VMEM_LIMIT_BYTES)


def _rms_scale(x, eps):
    return lax.rsqrt(jnp.mean(x * x, axis=-1, keepdims=True) + eps)


def _in_proj_kernel(x_ref, g_ref, w_ref, bg_ref, rope_ref,
                    u_ref, q_ref, k_ref, v_ref, gate_ref, *, d_model):
    x = x_ref[0]
    h = (x * _rms_scale(x, 1e-6) * g_ref[...]).astype(BF16)

    def proj(c0, n):
        return jnp.dot(h, w_ref[:, c0:c0 + n], preferred_element_type=F32)

    c_glu = 0
    c_q = 2 * d_model
    c_k = c_q + d_model
    c_v = c_k + d_model
    c_gate = c_v + d_model

    for c in range(0, d_model, COL_CHUNK):
        a = proj(c_glu + c, COL_CHUNK)
        g = proj(c_glu + d_model + c, COL_CHUNK)
        u_ref[0, :, c:c + COL_CHUNK] = (a * jax.nn.sigmoid(g)).astype(BF16)

    for base, out_ref, t0 in ((c_q, q_ref, 0), (c_k, k_ref, 3)):
        cos_t = rope_ref[t0]
        sin_a = rope_ref[t0 + 1]
        sin_b = rope_ref[t0 + 2]
        for c in range(0, d_model, COL_CHUNK):
            z = proj(base + c, COL_CHUNK)
            for hh in range(COL_CHUNK // LANES):
                t = z[:, hh * LANES:(hh + 1) * LANES]
                r = (t * cos_t
                     + pltpu.roll(t, LANES - ROT_DIM // 2, 1) * sin_a
                     + pltpu.roll(t, ROT_DIM // 2, 1) * sin_b)
                out_ref[0, c // LANES + hh] = r.astype(BF16)

    for c in range(0, d_model, COL_CHUNK):
        z = proj(c_v + c, COL_CHUNK)
        for hh in range(COL_CHUNK // LANES):
            v_ref[0, c // LANES + hh] = z[:, hh * LANES:(hh + 1) * LANES].astype(BF16)

    for c in range(0, 2 * d_model, COL_CHUNK):
        z = proj(c_gate + c, COL_CHUNK) + bg_ref[:, c:c + COL_CHUNK]
        gate_ref[0, :, c:c + COL_CHUNK] = jax.nn.sigmoid(z).astype(BF16)


def _in_proj(x, g_mix, w_in, b_gate, rope):
    bsz, seq, d = x.shape
    n_cols = w_in.shape[1]
    tm = ROW_TILE
    grid = (bsz, seq // tm)
    head_shape = jax.ShapeDtypeStruct((bsz, N_HEADS, seq, V_HEAD_DIM), BF16)
    head_spec = pl.BlockSpec((1, N_HEADS, tm, V_HEAD_DIM), lambda b, i: (b, 0, i, 0))
    return pl.pallas_call(
        functools.partial(_in_proj_kernel, d_model=d),
        grid=grid,
        in_specs=[
            pl.BlockSpec((1, tm, d), lambda b, i: (b, i, 0)),
            _const_spec((1, d)),
            _const_spec((d, n_cols)),
            _const_spec((1, 2 * d)),
            pl.BlockSpec((6, tm, LANES), lambda b, i: (0, i, 0)),
        ],
        out_specs=[
            pl.BlockSpec((1, tm, d), lambda b, i: (b, i, 0)),
            head_spec, head_spec, head_spec,
            pl.BlockSpec((1, tm, 2 * d), lambda b, i: (b, i, 0)),
        ],
        out_shape=[
            jax.ShapeDtypeStruct((bsz, seq, d), BF16),
            head_shape, head_shape, head_shape,
            jax.ShapeDtypeStruct((bsz, seq, 2 * d), BF16),
        ],
        compiler_params=_params(),
        name="in_proj",
    )(x, g_mix, w_in, b_gate, rope)


def _conv_kernel(u_ref, wdw_ref, bdw_ref, gln_ref, bln_ref, wpw_ref, gate_ref,
                 y_ref, win_ref, conv_ref):
    ts = CONV_TS
    si = pl.program_id(1)

    @pl.when(si == 0)
    def _():
        win_ref[0:CONV_HALO, :] = jnp.zeros((CONV_HALO, win_ref.shape[1]), F32)

    @pl.when(si > 0)
    def _():
        win_ref[0:CONV_HALO, :] = win_ref[ts:ts + CONV_HALO, :]

    win_ref[CONV_HALO:CONV_HALO + ts, :] = u_ref[0].astype(F32)

    off = CONV_HALO - (CONV_W - 1)
    for r0 in range(0, ts, CONV_RC):
        acc = jnp.zeros((CONV_RC, win_ref.shape[1]), F32)
        for j in range(CONV_W):
            acc = acc + win_ref[r0 + off + j:r0 + off + j + CONV_RC, :] * wdw_ref[j:j + 1, :]
        conv_ref[r0:r0 + CONV_RC, :] = acc + bdw_ref[...]

    c = conv_ref[...]
    mu = jnp.mean(c, axis=-1, keepdims=True)
    xc = c - mu
    var = jnp.mean(xc * xc, axis=-1, keepdims=True)
    y = xc * lax.rsqrt(var + 1e-5) * gln_ref[...] + bln_ref[...]
    y = y * jax.nn.sigmoid(y)
    yc = jnp.dot(y.astype(BF16), wpw_ref[...], preferred_element_type=F32)
    y_ref[0] = (gate_ref[0].astype(F32) * yc).astype(BF16)


def _conv_branch(u, w_dw, b_dw, g_ln, b_ln, w_pw, gates):
    bsz, seq, c = u.shape
    ts = CONV_TS
    return pl.pallas_call(
        _conv_kernel,
        grid=(bsz, seq // ts),
        in_specs=[
            pl.BlockSpec((1, ts, c), lambda b, s: (b, s, 0)),
            _const_spec((CONV_W, c)),
            _const_spec((1, c)),
            _const_spec((1, c)),
            _const_spec((1, c)),
            _const_spec((c, c)),
            pl.BlockSpec((1, ts, c), lambda b, s: (b, s, 0)),
        ],
        out_specs=pl.BlockSpec((1, ts, c), lambda b, s: (b, s, 0)),
        out_shape=jax.ShapeDtypeStruct((bsz, seq, c), BF16),
        scratch_shapes=[
            pltpu.VMEM((ts + CONV_HALO, c), F32),
            pltpu.VMEM((ts, c), F32),
        ],
        compiler_params=pltpu.CompilerParams(
            vmem_limit_bytes=VMEM_LIMIT_BYTES,
            dimension_semantics=("arbitrary", "arbitrary")),
        name="conv_branch",
    )(u, w_dw, b_dw, g_ln, b_ln, w_pw, gates)


def _attn_kernel(q_ref, k_ref, v_ref, lam_ref, gs_ref, o_ref,
                 s_ref, m_ref, l_ref, acc_ref, *, lambda_init):
    tq, tk = ATT_TQ, ATT_TK
    rows = 2 * tq
    i = pl.program_id(2)

    q = q_ref[0, 0]
    lane = lax.broadcasted_iota(jnp.int32, q.shape, 1)
    zero = jnp.zeros_like(q)
    q_st = jnp.concatenate([jnp.where(lane < HEAD_DIM, q, zero),
                            jnp.where(lane >= HEAD_DIM, q, zero)], axis=0)

    def scores(start):
        kj = k_ref[0, 0, pl.ds(start, tk), :]
        return lax.dot_general(q_st, kj, (((1,), (1,)), ((), ())),
                               preferred_element_type=F32)

    def fold(s):
        out = s[:, 0:LANES]
        for c in range(LANES, tk, LANES):
            out = jnp.maximum(out, s[:, c:c + LANES])
        return out

    m_ref[...] = jnp.full((rows, LANES), MASK_VALUE, F32)

    def body1(j, carry):
        start = pl.multiple_of(j * tk, tk)
        s = scores(start)
        s_ref[:, pl.ds(start, tk)] = s
        m_ref[...] = jnp.maximum(m_ref[...], fold(s))
        return carry

    lax.fori_loop(0, i, body1, 0)

    start = pl.multiple_of(i * tk, tk)
    s = scores(start)
    r_idx = lax.broadcasted_iota(jnp.int32, (rows, tk), 0)
    c_idx = lax.broadcasted_iota(jnp.int32, (rows, tk), 1)
    q_pos = jnp.where(r_idx >= tq, r_idx - tq, r_idx)
    s = jnp.where(c_idx <= q_pos, s, MASK_VALUE)
    s_ref[:, pl.ds(start, tk)] = s
    m = jnp.max(jnp.maximum(m_ref[...], fold(s)), axis=-1, keepdims=True)

    l_ref[...] = jnp.zeros((rows, LANES), F32)
    acc_ref[...] = jnp.zeros((rows, V_HEAD_DIM), F32)

    def body2(j, carry):
        start = pl.multiple_of(j * tk, tk)
        e = jnp.exp(s_ref[:, pl.ds(start, tk)] - m)
        esum = e[:, 0:LANES]
        for c in range(LANES, tk, LANES):
            esum = esum + e[:, c:c + LANES]
        l_ref[...] += esum
        vj = v_ref[0, 0, pl.ds(start, tk), :]
        acc_ref[...] += jnp.dot(e.astype(BF16), vj, preferred_element_type=F32)
        return carry

    lax.fori_loop(0, i + 1, body2, 0)

    l = jnp.sum(l_ref[...], axis=-1, keepdims=True)
    p = acc_ref[...] / l
    lam = (jnp.exp(jnp.sum(lam_ref[0:1, :] * lam_ref[1:2, :], axis=-1, keepdims=True))
           - jnp.exp(jnp.sum(lam_ref[2:3, :] * lam_ref[3:4, :], axis=-1, keepdims=True))
           + lambda_init)
    o = p[0:tq] - lam * p[tq:rows]
    o = o * _rms_scale(o, 1e-6) * gs_ref[...]
    o_ref[0] = (o * (1.0 - lambda_init)).astype(BF16)


def _attention(q, k, v, lam_vecs, g_subln, lambda_init):
    bsz, nh, seq, hd = q.shape
    tq = ATT_TQ
    kv_spec = pl.BlockSpec((1, 1, seq, hd), lambda b, h, i: (b, h, 0, 0))
    return pl.pallas_call(
        functools.partial(_attn_kernel, lambda_init=lambda_init),
        grid=(bsz, nh, seq // tq),
        in_specs=[
            pl.BlockSpec((1, 1, tq, hd), lambda b, h, i: (b, h, i, 0)),
            kv_spec, kv_spec,
            _const_spec((4, HEAD_DIM)),
            _const_spec((1, V_HEAD_DIM)),
        ],
        out_specs=pl.BlockSpec((1, tq, hd), lambda b, h, i: (b, i, h)),
        out_shape=jax.ShapeDtypeStruct((bsz, seq, nh * hd), BF16),
        scratch_shapes=[
            pltpu.VMEM((2 * tq, seq), F32),
            pltpu.VMEM((2 * tq, LANES), F32),
            pltpu.VMEM((2 * tq, LANES), F32),
            pltpu.VMEM((2 * tq, V_HEAD_DIM), F32),
        ],
        compiler_params=_params(),
        name="diff_attn",
    )(q, k, v, lam_vecs, g_subln)


def _out_ffn_kernel(x_ref, o_ref, yc_ref, ga_ref, wa_ref, wo_ref, gf_ref,
                    wg_ref, wu_ref, wd_ref, gfin_ref, out_ref, *, ff_chunks):
    ya = jnp.dot(o_ref[...], wa_ref[...], preferred_element_type=F32)
    merged = yc_ref[...].astype(F32) + ga_ref[...].astype(F32) * ya
    x1 = x_ref[...] + jnp.dot(merged.astype(BF16), wo_ref[...], preferred_element_type=F32)
    h2 = (x1 * _rms_scale(x1, 1e-6) * gf_ref[...]).astype(BF16)
    acc = x1
    for c0, n in ff_chunks:
        g = jnp.dot(h2, wg_ref[:, c0:c0 + n], preferred_element_type=F32)
        u = jnp.dot(h2, wu_ref[:, c0:c0 + n], preferred_element_type=F32)
        f = (g * jax.nn.sigmoid(g) * u).astype(BF16)
        acc = acc + jnp.dot(f, wd_ref[c0:c0 + n, :], preferred_element_type=F32)
    out_ref[...] = acc * _rms_scale(acc, 1e-6) * gfin_ref[...]


def _out_ffn(x2d, o2d, yc2d, gates2d, w_attn, w_out, g_ffn, w_g, w_u, w_d, g_final):
    t, d = x2d.shape
    d_ff = w_g.shape[1]
    tm = ROW_TILE
    assert d_ff % MXU_COLS == 0
    half = (d_ff // MXU_COLS + 1) // 2 * MXU_COLS
    ff_chunks = ((0, half), (half, d_ff - half))
    row = lambda i: (i, 0)
    return pl.pallas_call(
        functools.partial(_out_ffn_kernel, ff_chunks=ff_chunks),
        grid=(t // tm,),
        in_specs=[
            pl.BlockSpec((tm, d), row),
            pl.BlockSpec((tm, d), row),
            pl.BlockSpec((tm, d), row),
            pl.BlockSpec((tm, d), lambda i: (i, 1)),
            _const_spec((d, d)),
            _const_spec((d, d)),
            _const_spec((1, d)),
            _const_spec((d, d_ff)),
            _const_spec((d, d_ff)),
            _const_spec((d_ff, d)),
            _const_spec((1, d)),
        ],
        out_specs=pl.BlockSpec((tm, d), row),
        out_shape=jax.ShapeDtypeStruct((t, d), F32),
        compiler_params=_params(),
        name="out_ffn",
    )(x2d, o2d, yc2d, gates2d, w_attn, w_out, g_ffn, w_g, w_u, w_d, g_final)


def _rope_tables(seq):
    inv_freq = ROPE_THETA ** (-jnp.arange(0, ROT_DIM, 2, dtype=F32) / ROT_DIM)
    ang = jnp.arange(seq, dtype=F32)[:, None] * inv_freq[None, :]
    lane = np.arange(LANES)
    in_comp = lane % HEAD_DIM
    freq = lane % (ROT_DIM // 2)
    cos = jnp.cos(ang)[:, freq]
    sin = jnp.sin(ang)[:, freq]
    first = jnp.asarray(in_comp < ROT_DIM // 2)[None, :]
    second = jnp.asarray((in_comp >= ROT_DIM // 2) & (in_comp < ROT_DIM))[None, :]
    cos_t = jnp.where(first | second, cos, 1.0)
    sin_a = jnp.where(first, -sin, 0.0)
    sin_b = jnp.where(second, sin, 0.0)
    base = jnp.stack([cos_t, sin_a, sin_b])
    return jnp.concatenate([base * (HEAD_DIM ** -0.5), base], axis=0)


def kernel(x, g_mix, w_in, b_gate, w_dw, b_dw, g_conv_ln, b_conv_ln, w_conv_pw,
           lambda_q1, lambda_k1, lambda_q2, lambda_k2, g_subln, w_attn_pw, w_out,
           g_ffn, w_ffn_gate, w_ffn_up, w_ffn_down, g_final):
    bsz, seq, d = x.shape
    assert g_mix.shape[0] == 1, "one layer only"
    l = 0
    lambda_init = 0.8 - 0.6 * math.exp(-0.3 * l)
    t = bsz * seq
    rope = _rope_tables(seq)
    u, q, k, v, gates = _in_proj(x, g_mix[l][None], w_in[l].astype(BF16),
                                 b_gate[l][None], rope)
    yc = _conv_branch(u, w_dw[l], b_dw[l][None], g_conv_ln[l][None],
                      b_conv_ln[l][None], w_conv_pw[l].astype(BF16), gates)
    lam_vecs = jnp.stack([lambda_q1[l], lambda_k1[l], lambda_q2[l], lambda_k2[l]])
    o = _attention(q, k, v, lam_vecs, g_subln[l][None], lambda_init)
    out = _out_ffn(x.reshape(t, d), o.reshape(t, d), yc.reshape(t, d),
                   gates.reshape(t, 2 * d), w_attn_pw[l].astype(BF16),
                   w_out[l].astype(BF16), g_ffn[l][None],
                   w_ffn_gate[l].astype(BF16), w_ffn_up[l].astype(BF16),
                   w_ffn_down[l].astype(BF16), g_final[None])
    return out.reshape(bsz, seq, d)
```

```python
import functools
import math

import jax
import jax.numpy as jnp
import numpy as np
from jax import lax
from jax.experimental import pallas as pl
from jax.experimental.pallas import tpu as pltpu

F32 = jnp.float32
BF16 = jnp.bfloat16

N_HEADS = 8
HEAD_DIM = 64
V_HEAD_DIM = 2 * HEAD_DIM
ROT_DIM = HEAD_DIM // 4
ROPE_THETA = 500000.0
CONV_W = 31
LANES = 128
MXU_COLS = 256
VMEM_LIMIT_BYTES = 56 * 1024 * 1024

ROW_TILE = 512
COL_CHUNK = 512
ATT_TQ = 256
ATT_TK = 256
SUBLANES = 8
CONV_TS = 256
CONV_HALO = 32
CONV_SEG = CONV_TS // SUBLANES
CONV_PITCH = 68
CONV_GROUP = 8
MASK_VALUE = -1e30
LOG2_E = math.log2(math.e)


def _const_spec(shape):
    nd = len(shape)
    return pl.BlockSpec(shape, lambda *_: (0,) * nd, pipeline_mode=pl.Buffered(1))


def _params():
    return pltpu.CompilerParams(vmem_limit_bytes=VMEM_LIMIT_BYTES)


def _rms_scale(x, eps):
    return lax.rsqrt(jnp.mean(x * x, axis=-1, keepdims=True) + eps)


def _in_proj_kernel(x_ref, g_ref, w_ref, bg_ref, rope_ref,
                    u_ref, q_ref, k_ref, v_ref, gate_ref, *, d_model):
    x = x_ref[0]
    h = (x * _rms_scale(x, 1e-6) * g_ref[...]).astype(BF16)

    def proj(c0, n):
        return jnp.dot(h, w_ref[:, c0:c0 + n], preferred_element_type=F32)

    c_glu = 0
    c_q = 2 * d_model
    c_k = c_q + d_model
    c_v = c_k + d_model
    c_gate = c_v + d_model

    for c in range(0, d_model, COL_CHUNK):
        a = proj(c_glu + c, COL_CHUNK)
        g = proj(c_glu + d_model + c, COL_CHUNK)
        u_ref[0, :, c:c + COL_CHUNK] = (a * jax.nn.sigmoid(g)).astype(BF16)

    for base, out_ref, t0 in ((c_q, q_ref, 0), (c_k, k_ref, 3)):
        cos_t = rope_ref[t0]
        sin_a = rope_ref[t0 + 1]
        sin_b = rope_ref[t0 + 2]
        for c in range(0, d_model, COL_CHUNK):
            z = proj(base + c, COL_CHUNK)
            for hh in range(COL_CHUNK // LANES):
                t = z[:, hh * LANES:(hh + 1) * LANES]
                r = (t * cos_t
                     + pltpu.roll(t, LANES - ROT_DIM // 2, 1) * sin_a
                     + pltpu.roll(t, ROT_DIM // 2, 1) * sin_b)
                out_ref[0, c // LANES + hh] = r.astype(BF16)

    for c in range(0, d_model, COL_CHUNK):
        z = proj(c_v + c, COL_CHUNK)
        for hh in range(COL_CHUNK // LANES):
            v_ref[0, c // LANES + hh] = z[:, hh * LANES:(hh + 1) * LANES].astype(BF16)

    for c in range(0, 2 * d_model, COL_CHUNK):
        z = proj(c_gate + c, COL_CHUNK) + bg_ref[:, c:c + COL_CHUNK]
        gate_ref[0, :, c:c + COL_CHUNK] = jax.nn.sigmoid(z).astype(BF16)


def _in_proj(x, g_mix, w_in, b_gate, rope):
    bsz, seq, d = x.shape
    n_cols = w_in.shape[1]
    tm = ROW_TILE
    grid = (bsz, seq // tm)
    head_shape = jax.ShapeDtypeStruct((bsz, N_HEADS, seq, V_HEAD_DIM), BF16)
    head_spec = pl.BlockSpec((1, N_HEADS, tm, V_HEAD_DIM), lambda b, i: (b, 0, i, 0))
    return pl.pallas_call(
        functools.partial(_in_proj_kernel, d_model=d),
        grid=grid,
        in_specs=[
            pl.BlockSpec((1, tm, d), lambda b, i: (b, i, 0)),
            _const_spec((1, d)),
            _const_spec((d, n_cols)),
            _const_spec((1, 2 * d)),
            pl.BlockSpec((6, tm, LANES), lambda b, i: (0, i, 0)),
        ],
        out_specs=[
            pl.BlockSpec((1, tm, d), lambda b, i: (b, i, 0)),
            head_spec, head_spec, head_spec,
            pl.BlockSpec((1, tm, 2 * d), lambda b, i: (b, i, 0)),
        ],
        out_shape=[
            jax.ShapeDtypeStruct((bsz, seq, d), BF16),
            head_shape, head_shape, head_shape,
            jax.ShapeDtypeStruct((bsz, seq, 2 * d), BF16),
        ],
        compiler_params=_params(),
        name="in_proj",
    )(x, g_mix, w_in, b_gate, rope)


def _conv_kernel(u_ref, wdw_ref, bdw_ref, gln_ref, bln_ref, wpw_ref, gate_ref,
                 y_ref, win_ref, seg_ref, conv_ref):
    ts = CONV_TS
    n_ch = win_ref.shape[1]
    si = pl.program_id(1)

    @pl.when(si == 0)
    def _():
        win_ref[0:CONV_HALO, :] = jnp.zeros((CONV_HALO, n_ch), F32)

    @pl.when(si > 0)
    def _():
        win_ref[0:CONV_HALO, :] = win_ref[ts:ts + CONV_HALO, :]

    win_ref[CONV_HALO:CONV_HALO + ts, :] = u_ref[0].astype(F32)

    seg_rows = CONV_SEG + CONV_HALO
    for p in range(SUBLANES):
        seg = win_ref[CONV_SEG * p:CONV_SEG * p + seg_rows, :]
        for c in range(n_ch // LANES):
            seg_ref[c, CONV_PITCH * p:CONV_PITCH * p + seg_rows, :] = seg[:, c * LANES:(c + 1) * LANES]

    off = CONV_HALO - (CONV_W - 1)
    for c in range(n_ch // LANES):
        lanes = slice(c * LANES, (c + 1) * LANES)
        w = [jnp.broadcast_to(wdw_ref[j:j + 1, lanes], (SUBLANES, LANES)) for j in range(CONV_W)]
        bias = jnp.broadcast_to(bdw_ref[:, lanes], (SUBLANES, LANES))
        def group(n, carry, c=c, lanes=lanes, w=w, bias=bias):
            a0 = n * CONV_GROUP
            accs = [bias] * CONV_GROUP
            for r in range(CONV_GROUP + CONV_W - 1):
                row = seg_ref[c, pl.ds(a0 + off + r, SUBLANES, stride=CONV_PITCH), :]
                for g in range(CONV_GROUP):
                    j = r - g
                    if 0 <= j < CONV_W:
                        accs[g] = accs[g] + row * w[j]
            base = pl.multiple_of(a0 * SUBLANES, CONV_GROUP * SUBLANES)
            conv_ref[pl.ds(base, CONV_GROUP * SUBLANES), lanes] = jnp.concatenate(accs, axis=0)
            return carry

        lax.fori_loop(0, CONV_SEG // CONV_GROUP, group, 0)

    x = conv_ref[...]
    mu = jnp.mean(x, axis=-1, keepdims=True)
    xc = x - mu
    var = jnp.mean(xc * xc, axis=-1, keepdims=True)
    y = xc * lax.rsqrt(var + 1e-5) * gln_ref[...] + bln_ref[...]
    y = (y * jax.nn.sigmoid(y)).astype(BF16)

    n_idx = lax.broadcasted_iota(jnp.int32, (ts, ts), 0)
    m_idx = lax.broadcasted_iota(jnp.int32, (ts, ts), 1)
    src = SUBLANES * (n_idx % CONV_SEG) + n_idx // CONV_SEG
    perm = jnp.where(m_idx == src, 1.0, 0.0).astype(BF16)
    y = jnp.dot(perm, y, preferred_element_type=F32).astype(BF16)

    yc = jnp.dot(y, wpw_ref[...], preferred_element_type=F32)
    y_ref[0] = (gate_ref[0].astype(F32) * yc).astype(BF16)


def _conv_branch(u, w_dw, b_dw, g_ln, b_ln, w_pw, gates):
    bsz, seq, c = u.shape
    ts = CONV_TS
    return pl.pallas_call(
        _conv_kernel,
        grid=(bsz, seq // ts),
        in_specs=[
            pl.BlockSpec((1, ts, c), lambda b, s: (b, s, 0)),
            _const_spec((CONV_W, c)),
            _const_spec((1, c)),
            _const_spec((1, c)),
            _const_spec((1, c)),
            _const_spec((c, c)),
            pl.BlockSpec((1, ts, c), lambda b, s: (b, s, 0)),
        ],
        out_specs=pl.BlockSpec((1, ts, c), lambda b, s: (b, s, 0)),
        out_shape=jax.ShapeDtypeStruct((bsz, seq, c), BF16),
        scratch_shapes=[
            pltpu.VMEM((ts + CONV_HALO, c), F32),
            pltpu.VMEM((c // LANES, SUBLANES * CONV_PITCH, LANES), F32),
            pltpu.VMEM((ts, c), F32),
        ],
        compiler_params=pltpu.CompilerParams(
            vmem_limit_bytes=VMEM_LIMIT_BYTES,
            dimension_semantics=("arbitrary", "arbitrary")),
        name="conv_branch",
    )(u, w_dw, b_dw, g_ln, b_ln, w_pw, gates)


def _attn_kernel(q_ref, k_ref, v_ref, lam_ref, gs_ref, o_ref, s_ref, e_ref, m_ref, v1_ref,
                 *, lambda_init):
    tq, tk = ATT_TQ, ATT_TK
    rows = 2 * tq
    seq = q_ref.shape[2]

    lam = (jnp.exp(jnp.sum(lam_ref[0:1, :] * lam_ref[1:2, :], axis=-1, keepdims=True))
           - jnp.exp(jnp.sum(lam_ref[2:3, :] * lam_ref[3:4, :], axis=-1, keepdims=True))
           + lambda_init)

    r_idx = lax.broadcasted_iota(jnp.int32, (rows, tk), 0)
    c_idx = lax.broadcasted_iota(jnp.int32, (rows, tk), 1)
    causal = c_idx <= jnp.where(r_idx >= tq, r_idx - tq, r_idx)
    lane = lax.broadcasted_iota(jnp.int32, (tq, V_HEAD_DIM), 1)

    v1_ref[:, 0:V_HEAD_DIM] = v_ref[0, 0]
    v1_ref[:, V_HEAD_DIM:2 * V_HEAD_DIM] = jnp.ones((seq, V_HEAD_DIM), BF16)

    def fold(x, op):
        out = x[:, 0:LANES]
        for c in range(LANES, tk, LANES):
            out = op(out, x[:, c:c + LANES])
        return out

    def score_block(i):
        buf = i % 2
        q = q_ref[0, 0, i * tq:(i + 1) * tq, :]
        zero = jnp.zeros_like(q)
        q_st = jnp.concatenate([jnp.where(lane < HEAD_DIM, q, zero),
                                jnp.where(lane >= HEAD_DIM, q, zero)], axis=0)
        for j in range(i + 1):
            kj = k_ref[0, 0, j * tk:(j + 1) * tk, :]
            s = lax.dot_general(q_st, kj, (((1,), (1,)), ((), ())),
                                preferred_element_type=F32)
            if j == i:
                s = jnp.where(causal, s, MASK_VALUE)
            s_ref[buf, :, j * tk:(j + 1) * tk] = s
            f = fold(s, jnp.maximum)
            m_ref[buf] = f if j == 0 else jnp.maximum(m_ref[buf], f)

    order = list(range(seq // tq - 1, -1, -1))
    score_block(order[0])
    for n, i in enumerate(order):
        buf = i % 2
        if n + 1 < len(order):
            score_block(order[n + 1])
        m = jnp.max(m_ref[buf], axis=-1, keepdims=True)

        for j in range(i + 1):
            e = jnp.exp2(s_ref[buf, :, j * tk:(j + 1) * tk] - m)
            e_ref[buf, :, j * tk:(j + 1) * tk] = e.astype(BF16)

        kv = (i + 1) * tk
        acc = jnp.dot(e_ref[buf, :, 0:kv], v1_ref[0:kv, :], preferred_element_type=F32)
        p = acc[:, 0:V_HEAD_DIM] / acc[:, V_HEAD_DIM:2 * V_HEAD_DIM]
        o = p[0:tq] - lam * p[tq:rows]
        o = o * _rms_scale(o, 1e-6) * gs_ref[...]
        o_ref[0, i * tq:(i + 1) * tq, :] = (o * (1.0 - lambda_init)).astype(BF16)


def _attention(q, k, v, lam_vecs, g_subln, lambda_init):
    bsz, nh, seq, hd = q.shape
    head_spec = pl.BlockSpec((1, 1, seq, hd), lambda b, h: (b, h, 0, 0))
    return pl.pallas_call(
        functools.partial(_attn_kernel, lambda_init=lambda_init),
        grid=(bsz, nh),
        in_specs=[
            head_spec, head_spec, head_spec,
            _const_spec((4, HEAD_DIM)),
            _const_spec((1, V_HEAD_DIM)),
        ],
        out_specs=pl.BlockSpec((1, seq, hd), lambda b, h: (b, 0, h)),
        out_shape=jax.ShapeDtypeStruct((bsz, seq, nh * hd), BF16),
        scratch_shapes=[
            pltpu.VMEM((2, 2 * ATT_TQ, seq), F32),
            pltpu.VMEM((2, 2 * ATT_TQ, seq), BF16),
            pltpu.VMEM((2, 2 * ATT_TQ, LANES), F32),
            pltpu.VMEM((seq, 2 * V_HEAD_DIM), BF16),
        ],
        compiler_params=_params(),
        name="diff_attn",
    )(q, k, v, lam_vecs, g_subln)


def _out_ffn_kernel(x_ref, o_ref, yc_ref, ga_ref, wa_ref, wo_ref, gf_ref,
                    wg_ref, wu_ref, wd_ref, gfin_ref, out_ref, *, ff_chunks):
    ya = jnp.dot(o_ref[...], wa_ref[...], preferred_element_type=F32)
    merged = yc_ref[...].astype(F32) + ga_ref[...].astype(F32) * ya
    x1 = x_ref[...] + jnp.dot(merged.astype(BF16), wo_ref[...], preferred_element_type=F32)
    h2 = (x1 * _rms_scale(x1, 1e-6) * gf_ref[...]).astype(BF16)
    acc = x1
    for c0, n in ff_chunks:
        g = jnp.dot(h2, wg_ref[:, c0:c0 + n], preferred_element_type=F32)
        u = jnp.dot(h2, wu_ref[:, c0:c0 + n], preferred_element_type=F32)
        f = (g * jax.nn.sigmoid(g) * u).astype(BF16)
        acc = acc + jnp.dot(f, wd_ref[c0:c0 + n, :], preferred_element_type=F32)
    out_ref[...] = acc * _rms_scale(acc, 1e-6) * gfin_ref[...]


def _out_ffn(x2d, o2d, yc2d, gates2d, w_attn, w_out, g_ffn, w_g, w_u, w_d, g_final):
    t, d = x2d.shape
    d_ff = w_g.shape[1]
    tm = ROW_TILE
    assert d_ff % MXU_COLS == 0
    half = (d_ff // MXU_COLS + 1) // 2 * MXU_COLS
    ff_chunks = ((0, half), (half, d_ff - half))
    row = lambda i: (i, 0)
    return pl.pallas_call(
        functools.partial(_out_ffn_kernel, ff_chunks=ff_chunks),
        grid=(t // tm,),
        in_specs=[
            pl.BlockSpec((tm, d), row),
            pl.BlockSpec((tm, d), row),
            pl.BlockSpec((tm, d), row),
            pl.BlockSpec((tm, d), lambda i: (i, 1)),
            _const_spec((d, d)),
            _const_spec((d, d)),
            _const_spec((1, d)),
            _const_spec((d, d_ff)),
            _const_spec((d, d_ff)),
            _const_spec((d_ff, d)),
            _const_spec((1, d)),
        ],
        out_specs=pl.BlockSpec((tm, d), row),
        out_shape=jax.ShapeDtypeStruct((t, d), F32),
        compiler_params=_params(),
        name="out_ffn",
    )(x2d, o2d, yc2d, gates2d, w_attn, w_out, g_ffn, w_g, w_u, w_d, g_final)


def _rope_tables(seq):
    inv_freq = ROPE_THETA ** (-jnp.arange(0, ROT_DIM, 2, dtype=F32) / ROT_DIM)
    ang = jnp.arange(seq, dtype=F32)[:, None] * inv_freq[None, :]
    lane = np.arange(LANES)
    in_comp = lane % HEAD_DIM
    freq = lane % (ROT_DIM // 2)
    cos = jnp.cos(ang)[:, freq]
    sin = jnp.sin(ang)[:, freq]
    first = jnp.asarray(in_comp < ROT_DIM // 2)[None, :]
    second = jnp.asarray((in_comp >= ROT_DIM // 2) & (in_comp < ROT_DIM))[None, :]
    cos_t = jnp.where(first | second, cos, 1.0)
    sin_a = jnp.where(first, -sin, 0.0)
    sin_b = jnp.where(second, sin, 0.0)
    base = jnp.stack([cos_t, sin_a, sin_b])
    return jnp.concatenate([base * (HEAD_DIM ** -0.5 * LOG2_E), base], axis=0)


def kernel(x, g_mix, w_in, b_gate, w_dw, b_dw, g_conv_ln, b_conv_ln, w_conv_pw,
           lambda_q1, lambda_k1, lambda_q2, lambda_k2, g_subln, w_attn_pw, w_out,
           g_ffn, w_ffn_gate, w_ffn_up, w_ffn_down, g_final):
    bsz, seq, d = x.shape
    assert g_mix.shape[0] == 1, "one layer only"
    l = 0
    lambda_init = 0.8 - 0.6 * math.exp(-0.3 * l)
    t = bsz * seq
    rope = _rope_tables(seq)
    u, q, k, v, gates = _in_proj(x, g_mix[l][None], w_in[l].astype(BF16),
                                 b_gate[l][None], rope)
    yc = _conv_branch(u, w_dw[l], b_dw[l][None], g_conv_ln[l][None],
                      b_conv_ln[l][None], w_conv_pw[l].astype(BF16), gates)
    lam_vecs = jnp.stack([lambda_q1[l], lambda_k1[l], lambda_q2[l], lambda_k2[l]])
    o = _attention(q, k, v, lam_vecs, g_subln[l][None], lambda_init)
    out = _out_ffn(x.reshape(t, d), o.reshape(t, d), yc.reshape(t, d),
                   gates.reshape(t, 2 * d), w_attn_pw[l].astype(BF16),
                   w_out[l].astype(BF16), g_ffn[l][None],
                   w_ffn_gate[l].astype(BF16), w_ffn_up[l].astype(BF16),
                   w_ffn_down[l].astype(BF16), g_final[None])
    return out.reshape(bsz, seq, d)
```

```python
import functools
import math

import jax
import jax.numpy as jnp
import numpy as np
from jax import lax
from jax.experimental import pallas as pl
from jax.experimental.pallas import tpu as pltpu

F32 = jnp.float32
BF16 = jnp.bfloat16

N_HEADS = 8
HEAD_DIM = 64
V_HEAD_DIM = 2 * HEAD_DIM
ROT_DIM = HEAD_DIM // 4
ROPE_THETA = 500000.0
CONV_W = 31
LANES = 128
MXU_COLS = 256
VMEM_LIMIT_BYTES = 56 * 1024 * 1024

ROW_TILE = 512
COL_CHUNK = 512
ATT_TQ = 256
ATT_TK = 256
SUBLANES = 8
CONV_TS = 256
CONV_HALO = 32
PACKED_ROWS = 16
CONV_PITCH = 20
CONV_GROUP = 4
MASK_VALUE = -1e30
LOG2_E = math.log2(math.e)


def _const_spec(shape):
    nd = len(shape)
    return pl.BlockSpec(shape, lambda *_: (0,) * nd, pipeline_mode=pl.Buffered(1))


def _params():
    return pltpu.CompilerParams(vmem_limit_bytes=VMEM_LIMIT_BYTES)


def _rms_scale(x, eps):
    return lax.rsqrt(jnp.mean(x * x, axis=-1, keepdims=True) + eps)


def _in_proj_kernel(x_ref, g_ref, w_ref, bg_ref, rope_ref,
                    u_ref, q_ref, k_ref, v_ref, gate_ref, *, d_model):
    x = x_ref[0]
    h = (x * _rms_scale(x, 1e-6) * g_ref[...]).astype(BF16)

    def proj(c0, n):
        return jnp.dot(h, w_ref[:, c0:c0 + n], preferred_element_type=F32)

    c_glu = 0
    c_q = 2 * d_model
    c_k = c_q + d_model
    c_v = c_k + d_model
    c_gate = c_v + d_model

    for c in range(0, d_model, COL_CHUNK):
        a = proj(c_glu + c, COL_CHUNK)
        g = proj(c_glu + d_model + c, COL_CHUNK)
        u_ref[0, :, c:c + COL_CHUNK] = (a * jax.nn.sigmoid(g)).astype(BF16)

    for base, out_ref, t0 in ((c_q, q_ref, 0), (c_k, k_ref, 3)):
        cos_t = rope_ref[t0]
        sin_a = rope_ref[t0 + 1]
        sin_b = rope_ref[t0 + 2]
        for c in range(0, d_model, COL_CHUNK):
            z = proj(base + c, COL_CHUNK)
            for hh in range(COL_CHUNK // LANES):
                t = z[:, hh * LANES:(hh + 1) * LANES]
                r = (t * cos_t
                     + pltpu.roll(t, LANES - ROT_DIM // 2, 1) * sin_a
                     + pltpu.roll(t, ROT_DIM // 2, 1) * sin_b)
                out_ref[0, c // LANES + hh] = r.astype(BF16)

    for c in range(0, d_model, COL_CHUNK):
        z = proj(c_v + c, COL_CHUNK)
        for hh in range(COL_CHUNK // LANES):
            v_ref[0, c // LANES + hh] = z[:, hh * LANES:(hh + 1) * LANES].astype(BF16)

    for c in range(0, 2 * d_model, COL_CHUNK):
        z = proj(c_gate + c, COL_CHUNK) + bg_ref[:, c:c + COL_CHUNK]
        gate_ref[0, :, c:c + COL_CHUNK] = jax.nn.sigmoid(z).astype(BF16)


def _in_proj(x, g_mix, w_in, b_gate, rope):
    bsz, seq, d = x.shape
    n_cols = w_in.shape[1]
    tm = ROW_TILE
    grid = (bsz, seq // tm)
    head_shape = jax.ShapeDtypeStruct((bsz, N_HEADS, seq, V_HEAD_DIM), BF16)
    head_spec = pl.BlockSpec((1, N_HEADS, tm, V_HEAD_DIM), lambda b, i: (b, 0, i, 0))
    return pl.pallas_call(
        functools.partial(_in_proj_kernel, d_model=d),
        grid=grid,
        in_specs=[
            pl.BlockSpec((1, tm, d), lambda b, i: (b, i, 0)),
            _const_spec((1, d)),
            _const_spec((d, n_cols)),
            _const_spec((1, 2 * d)),
            pl.BlockSpec((6, tm, LANES), lambda b, i: (0, i, 0)),
        ],
        out_specs=[
            pl.BlockSpec((1, tm, d), lambda b, i: (b, i, 0)),
            head_spec, head_spec, head_spec,
            pl.BlockSpec((1, tm, 2 * d), lambda b, i: (b, i, 0)),
        ],
        out_shape=[
            jax.ShapeDtypeStruct((bsz, seq, d), BF16),
            head_shape, head_shape, head_shape,
            jax.ShapeDtypeStruct((bsz, seq, 2 * d), BF16),
        ],
        compiler_params=_params(),
        name="in_proj",
    )(x, g_mix, w_in, b_gate, rope)


def _conv_kernel(u_ref, wdw_ref, bdw_ref, gln_ref, bln_ref, wpw_ref, gate_ref,
                 y_ref, tail_ref, seg_ref, conv_ref):
    ts = CONV_TS
    n_ch = tail_ref.shape[1]
    n_slabs = n_ch // LANES
    halo_words = CONV_HALO // 2
    n_vregs = (CONV_HALO + ts) // PACKED_ROWS
    copy_base = SUBLANES * CONV_PITCH

    @pl.when(pl.program_id(1) == 0)
    def _():
        tail_ref[...] = jnp.zeros((halo_words, n_ch), jnp.uint32)

    words = pltpu.bitcast(u_ref[0], jnp.uint32)
    window = [tail_ref[SUBLANES * v:SUBLANES * (v + 1), :] for v in range(halo_words // SUBLANES)]
    window += [words[SUBLANES * v:SUBLANES * (v + 1), :] for v in range(ts // PACKED_ROWS)]
    window.append(jnp.zeros((SUBLANES, n_ch), jnp.uint32))
    for v, rows in enumerate(window):
        for c in range(n_slabs):
            piece = rows[:, c * LANES:(c + 1) * LANES]
            if v < n_vregs:
                seg_ref[0, c, pl.ds(v, SUBLANES, stride=CONV_PITCH), :] = piece
            if v > 0:
                seg_ref[0, c, pl.ds(copy_base + v - 1, SUBLANES, stride=CONV_PITCH), :] = piece
    tail_ref[...] = words[ts // 2 - halo_words:ts // 2, :]

    for v in range(n_vregs):
        for c in range(n_slabs):
            lo = seg_ref[0, c, pl.ds(v, SUBLANES, stride=CONV_PITCH), :]
            hi = seg_ref[0, c, pl.ds(CONV_PITCH + v, SUBLANES, stride=CONV_PITCH), :]
            piece = (lo >> 16) | (hi << 16)
            seg_ref[1, c, pl.ds(v, SUBLANES, stride=CONV_PITCH), :] = piece
            if v > 0:
                seg_ref[1, c, pl.ds(copy_base + v - 1, SUBLANES, stride=CONV_PITCH), :] = piece

    lead = CONV_HALO - (CONV_W - 1)
    for c in range(n_slabs):
        lanes = slice(c * LANES, (c + 1) * LANES)
        bias = jnp.broadcast_to(bdw_ref[:, lanes], (PACKED_ROWS, LANES))

        def group(n, carry, c=c, lanes=lanes, bias=bias):
            a0 = n * CONV_GROUP
            w = [wdw_ref[c, j] for j in range(CONV_W)]
            loaded = {}
            outs = []
            for g in range(CONV_GROUP):
                acc = None
                for j in range(CONV_W):
                    start = lead + j
                    word = start // 2 + g * SUBLANES
                    key = (start % 2, word)
                    if key not in loaded:
                        q, r = divmod(word, SUBLANES)
                        raw = seg_ref[start % 2, c,
                                      pl.ds(r * CONV_PITCH + a0 + q, SUBLANES, stride=CONV_PITCH), :]
                        loaded[key] = pltpu.bitcast(raw, BF16)
                    term = loaded[key].astype(F32) * w[j].astype(F32)
                    acc = term if acc is None else acc + term
                outs.append(acc + bias)
            base = pl.multiple_of(a0 * PACKED_ROWS, CONV_GROUP * PACKED_ROWS)
            conv_ref[pl.ds(base, CONV_GROUP * PACKED_ROWS), lanes] = jnp.concatenate(outs, axis=0)
            return carry

        lax.fori_loop(0, ts // PACKED_ROWS // CONV_GROUP, group, 0)

    x = conv_ref[...]
    mu = jnp.mean(x, axis=-1, keepdims=True)
    xc = x - mu
    var = jnp.mean(xc * xc, axis=-1, keepdims=True)
    y = xc * lax.rsqrt(var + 1e-5) * gln_ref[...] + bln_ref[...]
    y = (y * jax.nn.sigmoid(y)).astype(BF16)
    yc = jnp.dot(y, wpw_ref[...], preferred_element_type=F32)
    y_ref[0] = (gate_ref[0].astype(F32) * yc).astype(BF16)


def _conv_branch(u, w_dw, b_dw, g_ln, b_ln, w_pw, gates):
    bsz, seq, c = u.shape
    ts = CONV_TS
    return pl.pallas_call(
        _conv_kernel,
        grid=(bsz, seq // ts),
        in_specs=[
            pl.BlockSpec((1, ts, c), lambda b, s: (b, s, 0)),
            _const_spec((c // LANES, CONV_W, PACKED_ROWS, LANES)),
            _const_spec((1, c)),
            _const_spec((1, c)),
            _const_spec((1, c)),
            _const_spec((c, c)),
            pl.BlockSpec((1, ts, c), lambda b, s: (b, s, 0)),
        ],
        out_specs=pl.BlockSpec((1, ts, c), lambda b, s: (b, s, 0)),
        out_shape=jax.ShapeDtypeStruct((bsz, seq, c), BF16),
        scratch_shapes=[
            pltpu.VMEM((CONV_HALO // 2, c), jnp.uint32),
            pltpu.VMEM((2, c // LANES, 2 * SUBLANES * CONV_PITCH, LANES), jnp.uint32),
            pltpu.VMEM((ts, c), F32),
        ],
        compiler_params=pltpu.CompilerParams(
            vmem_limit_bytes=VMEM_LIMIT_BYTES,
            dimension_semantics=("arbitrary", "arbitrary")),
        name="conv_branch",
    )(u, w_dw, b_dw, g_ln, b_ln, w_pw, gates)


def _attn_kernel(q_ref, k_ref, v_ref, lam_ref, gs_ref, o_ref, s_ref, e_ref, m_ref, v1_ref,
                 *, lambda_init):
    tq, tk = ATT_TQ, ATT_TK
    rows = 2 * tq
    seq = q_ref.shape[2]

    lam = (jnp.exp(jnp.sum(lam_ref[0:1, :] * lam_ref[1:2, :], axis=-1, keepdims=True))
           - jnp.exp(jnp.sum(lam_ref[2:3, :] * lam_ref[3:4, :], axis=-1, keepdims=True))
           + lambda_init)

    r_idx = lax.broadcasted_iota(jnp.int32, (rows, tk), 0)
    c_idx = lax.broadcasted_iota(jnp.int32, (rows, tk), 1)
    causal = c_idx <= jnp.where(r_idx >= tq, r_idx - tq, r_idx)
    lane = lax.broadcasted_iota(jnp.int32, (tq, V_HEAD_DIM), 1)

    v1_ref[:, 0:V_HEAD_DIM] = v_ref[0, 0]
    v1_ref[:, V_HEAD_DIM:2 * V_HEAD_DIM] = jnp.ones((seq, V_HEAD_DIM), BF16)

    def fold(x, op):
        out = x[:, 0:LANES]
        for c in range(LANES, tk, LANES):
            out = op(out, x[:, c:c + LANES])
        return out

    def score_block(i):
        buf = i % 2
        q = q_ref[0, 0, i * tq:(i + 1) * tq, :]
        zero = jnp.zeros_like(q)
        q_st = jnp.concatenate([jnp.where(lane < HEAD_DIM, q, zero),
                                jnp.where(lane >= HEAD_DIM, q, zero)], axis=0)
        for j in range(i + 1):
            kj = k_ref[0, 0, j * tk:(j + 1) * tk, :]
            s = lax.dot_general(q_st, kj, (((1,), (1,)), ((), ())),
                                preferred_element_type=F32)
            if j == i:
                s = jnp.where(causal, s, MASK_VALUE)
            s_ref[buf, :, j * tk:(j + 1) * tk] = s
            f = fold(s, jnp.maximum)
            m_ref[buf] = f if j == 0 else jnp.maximum(m_ref[buf], f)

    order = list(range(seq // tq - 1, -1, -1))
    score_block(order[0])
    for n, i in enumerate(order):
        buf = i % 2
        if n + 1 < len(order):
            score_block(order[n + 1])
        m = jnp.max(m_ref[buf], axis=-1, keepdims=True)

        for j in range(i + 1):
            e = jnp.exp2(s_ref[buf, :, j * tk:(j + 1) * tk] - m)
            e_ref[buf, :, j * tk:(j + 1) * tk] = e.astype(BF16)

        kv = (i + 1) * tk
        acc = jnp.dot(e_ref[buf, :, 0:kv], v1_ref[0:kv, :], preferred_element_type=F32)
        p = acc[:, 0:V_HEAD_DIM] / acc[:, V_HEAD_DIM:2 * V_HEAD_DIM]
        o = p[0:tq] - lam * p[tq:rows]
        o = o * _rms_scale(o, 1e-6) * gs_ref[...]
        o_ref[0, i * tq:(i + 1) * tq, :] = (o * (1.0 - lambda_init)).astype(BF16)


def _attention(q, k, v, lam_vecs, g_subln, lambda_init):
    bsz, nh, seq, hd = q.shape
    head_spec = pl.BlockSpec((1, 1, seq, hd), lambda b, h: (b, h, 0, 0))
    return pl.pallas_call(
        functools.partial(_attn_kernel, lambda_init=lambda_init),
        grid=(bsz, nh),
        in_specs=[
            head_spec, head_spec, head_spec,
            _const_spec((4, HEAD_DIM)),
            _const_spec((1, V_HEAD_DIM)),
        ],
        out_specs=pl.BlockSpec((1, seq, hd), lambda b, h: (b, 0, h)),
        out_shape=jax.ShapeDtypeStruct((bsz, seq, nh * hd), BF16),
        scratch_shapes=[
            pltpu.VMEM((2, 2 * ATT_TQ, seq), F32),
            pltpu.VMEM((2, 2 * ATT_TQ, seq), BF16),
            pltpu.VMEM((2, 2 * ATT_TQ, LANES), F32),
            pltpu.VMEM((seq, 2 * V_HEAD_DIM), BF16),
        ],
        compiler_params=_params(),
        name="diff_attn",
    )(q, k, v, lam_vecs, g_subln)


def _out_ffn_kernel(x_ref, o_ref, yc_ref, ga_ref, wa_ref, wo_ref, gf_ref,
                    wg_ref, wu_ref, wd_ref, gfin_ref, out_ref, *, ff_chunks):
    ya = jnp.dot(o_ref[...], wa_ref[...], preferred_element_type=F32)
    merged = yc_ref[...].astype(F32) + ga_ref[...].astype(F32) * ya
    x1 = x_ref[...] + jnp.dot(merged.astype(BF16), wo_ref[...], preferred_element_type=F32)
    h2 = (x1 * _rms_scale(x1, 1e-6) * gf_ref[...]).astype(BF16)
    acc = x1
    for c0, n in ff_chunks:
        g = jnp.dot(h2, wg_ref[:, c0:c0 + n], preferred_element_type=F32)
        u = jnp.dot(h2, wu_ref[:, c0:c0 + n], preferred_element_type=F32)
        f = (g * jax.nn.sigmoid(g) * u).astype(BF16)
        acc = acc + jnp.dot(f, wd_ref[c0:c0 + n, :], preferred_element_type=F32)
    out_ref[...] = acc * _rms_scale(acc, 1e-6) * gfin_ref[...]


def _out_ffn(x2d, o2d, yc2d, gates2d, w_attn, w_out, g_ffn, w_g, w_u, w_d, g_final):
    t, d = x2d.shape
    d_ff = w_g.shape[1]
    tm = ROW_TILE
    assert d_ff % MXU_COLS == 0
    half = (d_ff // MXU_COLS + 1) // 2 * MXU_COLS
    ff_chunks = ((0, half), (half, d_ff - half))
    row = lambda i: (i, 0)
    return pl.pallas_call(
        functools.partial(_out_ffn_kernel, ff_chunks=ff_chunks),
        grid=(t // tm,),
        in_specs=[
            pl.BlockSpec((tm, d), row),
            pl.BlockSpec((tm, d), row),
            pl.BlockSpec((tm, d), row),
            pl.BlockSpec((tm, d), lambda i: (i, 1)),
            _const_spec((d, d)),
            _const_spec((d, d)),
            _const_spec((1, d)),
            _const_spec((d, d_ff)),
            _const_spec((d, d_ff)),
            _const_spec((d_ff, d)),
            _const_spec((1, d)),
        ],
        out_specs=pl.BlockSpec((tm, d), row),
        out_shape=jax.ShapeDtypeStruct((t, d), F32),
        compiler_params=_params(),
        name="out_ffn",
    )(x2d, o2d, yc2d, gates2d, w_attn, w_out, g_ffn, w_g, w_u, w_d, g_final)


def _rope_tables(seq):
    inv_freq = ROPE_THETA ** (-jnp.arange(0, ROT_DIM, 2, dtype=F32) / ROT_DIM)
    ang = jnp.arange(seq, dtype=F32)[:, None] * inv_freq[None, :]
    lane = np.arange(LANES)
    in_comp = lane % HEAD_DIM
    freq = lane % (ROT_DIM // 2)
    cos = jnp.cos(ang)[:, freq]
    sin = jnp.sin(ang)[:, freq]
    first = jnp.asarray(in_comp < ROT_DIM // 2)[None, :]
    second = jnp.asarray((in_comp >= ROT_DIM // 2) & (in_comp < ROT_DIM))[None, :]
    cos_t = jnp.where(first | second, cos, 1.0)
    sin_a = jnp.where(first, -sin, 0.0)
    sin_b = jnp.where(second, sin, 0.0)
    base = jnp.stack([cos_t, sin_a, sin_b])
    return jnp.concatenate([base * (HEAD_DIM ** -0.5 * LOG2_E), base], axis=0)


def kernel(x, g_mix, w_in, b_gate, w_dw, b_dw, g_conv_ln, b_conv_ln, w_conv_pw,
           lambda_q1, lambda_k1, lambda_q2, lambda_k2, g_subln, w_attn_pw, w_out,
           g_ffn, w_ffn_gate, w_ffn_up, w_ffn_down, g_final):
    bsz, seq, d = x.shape
    assert g_mix.shape[0] == 1, "one layer only"
    l = 0
    lambda_init = 0.8 - 0.6 * math.exp(-0.3 * l)
    t = bsz * seq
    rope = _rope_tables(seq)
    u, q, k, v, gates = _in_proj(x, g_mix[l][None], w_in[l].astype(BF16),
                                 b_gate[l][None], rope)
    w_taps = jnp.broadcast_to(
        w_dw[l].astype(BF16).reshape(CONV_W, 1, d // LANES, LANES).transpose(2, 0, 1, 3),
        (d // LANES, CONV_W, PACKED_ROWS, LANES))
    yc = _conv_branch(u, w_taps, b_dw[l][None], g_conv_ln[l][None],
                      b_conv_ln[l][None], w_conv_pw[l].astype(BF16), gates)
    lam_vecs = jnp.stack([lambda_q1[l], lambda_k1[l], lambda_q2[l], lambda_k2[l]])
    o = _attention(q, k, v, lam_vecs, g_subln[l][None], lambda_init)
    out = _out_ffn(x.reshape(t, d), o.reshape(t, d), yc.reshape(t, d),
                   gates.reshape(t, 2 * d), w_attn_pw[l].astype(BF16),
                   w_out[l].astype(BF16), g_ffn[l][None],
                   w_ffn_gate[l].astype(BF16), w_ffn_up[l].astype(BF16),
                   w_ffn_down[l].astype(BF16), g_final[None])
    return out.reshape(bsz, seq, d)
```

```python
import functools
import math

import jax
import jax.numpy as jnp
import numpy as np
from jax import lax
from jax.experimental import pallas as pl
from jax.experimental.pallas import tpu as pltpu

F32 = jnp.float32
BF16 = jnp.bfloat16

N_HEADS = 8
HEAD_DIM = 64
V_HEAD_DIM = 2 * HEAD_DIM
ROT_DIM = HEAD_DIM // 4
ROPE_THETA = 500000.0
CONV_W = 31
LANES = 128
MXU_COLS = 256
VMEM_LIMIT_BYTES = 56 * 1024 * 1024

ROW_TILE = 512
COL_CHUNK = 512
ATT_TQ = 256
ATT_TK = 256
ATT_BUFS = 3
SUBLANES = 8
CONV_TS = 256
CONV_HALO = 32
PACKED_ROWS = 16
CONV_PITCH = 20
CONV_GROUP = 8
MASK_VALUE = -1e30
LOG2_E = math.log2(math.e)


def _const_spec(shape):
    nd = len(shape)
    return pl.BlockSpec(shape, lambda *_: (0,) * nd, pipeline_mode=pl.Buffered(1))


def _params():
    return pltpu.CompilerParams(vmem_limit_bytes=VMEM_LIMIT_BYTES)


def _rms_scale(x, eps):
    return lax.rsqrt(jnp.mean(x * x, axis=-1, keepdims=True) + eps)


def _in_proj_kernel(x_ref, g_ref, w_ref, bg_ref, rope_ref,
                    u_ref, q_ref, k_ref, v_ref, gate_ref, *, d_model):
    x = x_ref[0]
    row_scale = _rms_scale(x, 1e-6)
    h = (x * g_ref[...]).astype(BF16)

    def proj(c0, n):
        return jnp.dot(h, w_ref[:, c0:c0 + n], preferred_element_type=F32) * row_scale

    c_glu = 0
    c_q = 2 * d_model
    c_k = c_q + d_model
    c_v = c_k + d_model
    c_gate = c_v + d_model

    for c in range(0, 2 * d_model, COL_CHUNK):
        z = proj(c_gate + c, COL_CHUNK) + bg_ref[:, c:c + COL_CHUNK]
        gate_ref[0, :, c:c + COL_CHUNK] = jax.nn.sigmoid(z).astype(BF16)

    for c in range(0, d_model, COL_CHUNK):
        a = proj(c_glu + c, COL_CHUNK)
        g = proj(c_glu + d_model + c, COL_CHUNK)
        u_ref[0, :, c:c + COL_CHUNK] = (a * jax.nn.sigmoid(g)).astype(BF16)

    for base, out_ref, t0 in ((c_q, q_ref, 0), (c_k, k_ref, 3)):
        cos_t = rope_ref[t0]
        sin_a = rope_ref[t0 + 1]
        sin_b = rope_ref[t0 + 2]
        for c in range(0, d_model, COL_CHUNK):
            z = proj(base + c, COL_CHUNK)
            for hh in range(COL_CHUNK // LANES):
                t = z[:, hh * LANES:(hh + 1) * LANES]
                r = (t * cos_t
                     + pltpu.roll(t, LANES - ROT_DIM // 2, 1) * sin_a
                     + pltpu.roll(t, ROT_DIM // 2, 1) * sin_b)
                out_ref[0, c // LANES + hh] = r.astype(BF16)

    for c in range(0, d_model, COL_CHUNK):
        z = proj(c_v + c, COL_CHUNK)
        for hh in range(COL_CHUNK // LANES):
            v_ref[0, c // LANES + hh] = z[:, hh * LANES:(hh + 1) * LANES].astype(BF16)


def _in_proj(x, g_mix, w_in, b_gate, rope):
    bsz, seq, d = x.shape
    n_cols = w_in.shape[1]
    tm = ROW_TILE
    grid = (bsz, seq // tm)
    head_shape = jax.ShapeDtypeStruct((bsz, N_HEADS, seq, V_HEAD_DIM), BF16)
    head_spec = pl.BlockSpec((1, N_HEADS, tm, V_HEAD_DIM), lambda b, i: (b, 0, i, 0))
    return pl.pallas_call(
        functools.partial(_in_proj_kernel, d_model=d),
        grid=grid,
        in_specs=[
            pl.BlockSpec((1, tm, d), lambda b, i: (b, i, 0)),
            _const_spec((1, d)),
            _const_spec((d, n_cols)),
            _const_spec((1, 2 * d)),
            pl.BlockSpec((6, tm, LANES), lambda b, i: (0, i, 0)),
        ],
        out_specs=[
            pl.BlockSpec((1, tm, d), lambda b, i: (b, i, 0)),
            head_spec, head_spec, head_spec,
            pl.BlockSpec((1, tm, 2 * d), lambda b, i: (b, i, 0)),
        ],
        out_shape=[
            jax.ShapeDtypeStruct((bsz, seq, d), BF16),
            head_shape, head_shape, head_shape,
            jax.ShapeDtypeStruct((bsz, seq, 2 * d), BF16),
        ],
        compiler_params=_params(),
        name="in_proj",
    )(x, g_mix, w_in, b_gate, rope)


def _conv_kernel(u_ref, wdw_ref, bdw_ref, gln_ref, bln_ref, wpw_ref, gate_ref,
                 y_ref, tail_ref, seg_ref, conv_ref):
    ts = CONV_TS
    n_ch = tail_ref.shape[1]
    n_slabs = n_ch // LANES
    halo_words = CONV_HALO // 2
    n_vregs = (CONV_HALO + ts) // PACKED_ROWS
    copy_base = SUBLANES * CONV_PITCH

    @pl.when(pl.program_id(1) == 0)
    def _():
        tail_ref[...] = jnp.zeros((halo_words, n_ch), jnp.uint32)

    words = pltpu.bitcast(u_ref[0], jnp.uint32)
    window = [tail_ref[SUBLANES * v:SUBLANES * (v + 1), :] for v in range(halo_words // SUBLANES)]
    window += [words[SUBLANES * v:SUBLANES * (v + 1), :] for v in range(ts // PACKED_ROWS)]
    window.append(jnp.zeros((SUBLANES, n_ch), jnp.uint32))
    for v, rows in enumerate(window):
        for c in range(n_slabs):
            piece = rows[:, c * LANES:(c + 1) * LANES]
            if v < n_vregs:
                seg_ref[0, c, pl.ds(v, SUBLANES, stride=CONV_PITCH), :] = piece
            if v > 0:
                seg_ref[0, c, pl.ds(copy_base + v - 1, SUBLANES, stride=CONV_PITCH), :] = piece
    tail_ref[...] = words[ts // 2 - halo_words:ts // 2, :]

    for v in range(n_vregs):
        for c in range(n_slabs):
            lo = seg_ref[0, c, pl.ds(v, SUBLANES, stride=CONV_PITCH), :]
            hi = seg_ref[0, c, pl.ds(CONV_PITCH + v, SUBLANES, stride=CONV_PITCH), :]
            piece = (lo >> 16) | (hi << 16)
            seg_ref[1, c, pl.ds(v, SUBLANES, stride=CONV_PITCH), :] = piece
            if v > 0:
                seg_ref[1, c, pl.ds(copy_base + v - 1, SUBLANES, stride=CONV_PITCH), :] = piece

    lead = CONV_HALO - (CONV_W - 1)
    for c in range(n_slabs):
        lanes = slice(c * LANES, (c + 1) * LANES)
        bias = jnp.broadcast_to(bdw_ref[:, lanes], (PACKED_ROWS, LANES))

        def group(n, carry, c=c, lanes=lanes, bias=bias):
            a0 = n * CONV_GROUP
            w = [wdw_ref[c, j] for j in range(CONV_W)]
            loaded = {}
            outs = []
            for g in range(CONV_GROUP):
                acc = None
                for j in range(CONV_W):
                    start = lead + j
                    word = start // 2 + g * SUBLANES
                    key = (start % 2, word)
                    if key not in loaded:
                        q, r = divmod(word, SUBLANES)
                        raw = seg_ref[start % 2, c,
                                      pl.ds(r * CONV_PITCH + a0 + q, SUBLANES, stride=CONV_PITCH), :]
                        loaded[key] = pltpu.bitcast(raw, BF16)
                    term = loaded[key].astype(F32) * w[j].astype(F32)
                    acc = term if acc is None else acc + term
                outs.append(acc + bias)
            base = pl.multiple_of(a0 * PACKED_ROWS, CONV_GROUP * PACKED_ROWS)
            conv_ref[pl.ds(base, CONV_GROUP * PACKED_ROWS), lanes] = jnp.concatenate(outs, axis=0)
            return carry

        lax.fori_loop(0, ts // PACKED_ROWS // CONV_GROUP, group, 0)

    x = conv_ref[...]
    mu = jnp.mean(x, axis=-1, keepdims=True)
    xc = x - mu
    var = jnp.mean(xc * xc, axis=-1, keepdims=True)
    y = xc * lax.rsqrt(var + 1e-5) * gln_ref[...] + bln_ref[...]
    y = (y * jax.nn.sigmoid(y)).astype(BF16)
    yc = jnp.dot(y, wpw_ref[...], preferred_element_type=F32)
    y_ref[0] = (gate_ref[0].astype(F32) * yc).astype(BF16)


def _conv_branch(u, w_dw, b_dw, g_ln, b_ln, w_pw, gates):
    bsz, seq, c = u.shape
    ts = CONV_TS
    return pl.pallas_call(
        _conv_kernel,
        grid=(bsz, seq // ts),
        in_specs=[
            pl.BlockSpec((1, ts, c), lambda b, s: (b, s, 0)),
            _const_spec((c // LANES, CONV_W, PACKED_ROWS, LANES)),
            _const_spec((1, c)),
            _const_spec((1, c)),
            _const_spec((1, c)),
            _const_spec((c, c)),
            pl.BlockSpec((1, ts, c), lambda b, s: (b, s, 0)),
        ],
        out_specs=pl.BlockSpec((1, ts, c), lambda b, s: (b, s, 0)),
        out_shape=jax.ShapeDtypeStruct((bsz, seq, c), BF16),
        scratch_shapes=[
            pltpu.VMEM((CONV_HALO // 2, c), jnp.uint32),
            pltpu.VMEM((2, c // LANES, 2 * SUBLANES * CONV_PITCH, LANES), jnp.uint32),
            pltpu.VMEM((ts, c), F32),
        ],
        compiler_params=pltpu.CompilerParams(
            vmem_limit_bytes=VMEM_LIMIT_BYTES,
            dimension_semantics=("arbitrary", "arbitrary")),
        name="conv_branch",
    )(u, w_dw, b_dw, g_ln, b_ln, w_pw, gates)


def _attn_kernel(q_ref, k_ref, v_ref, lam_ref, gs_ref, o_ref, s_ref, e_ref, m_ref, v1_ref,
                 *, lambda_init):
    tq, tk = ATT_TQ, ATT_TK
    rows = 2 * tq
    seq = q_ref.shape[2]

    lam = (jnp.exp(jnp.sum(lam_ref[0:1, :] * lam_ref[1:2, :], axis=-1, keepdims=True))
           - jnp.exp(jnp.sum(lam_ref[2:3, :] * lam_ref[3:4, :], axis=-1, keepdims=True))
           + lambda_init)

    r_idx = lax.broadcasted_iota(jnp.int32, (rows, tk), 0)
    c_idx = lax.broadcasted_iota(jnp.int32, (rows, tk), 1)
    causal = c_idx <= jnp.where(r_idx >= tq, r_idx - tq, r_idx)
    lane = lax.broadcasted_iota(jnp.int32, (tq, V_HEAD_DIM), 1)

    v1_ref[:, 0:V_HEAD_DIM] = v_ref[0, 0]
    v1_ref[:, V_HEAD_DIM:2 * V_HEAD_DIM] = jnp.ones((seq, V_HEAD_DIM), BF16)

    def fold(x, op):
        out = x[:, 0:LANES]
        for c in range(LANES, tk, LANES):
            out = op(out, x[:, c:c + LANES])
        return out

    def score_block(i):
        buf = i % ATT_BUFS
        q = q_ref[0, 0, i * tq:(i + 1) * tq, :]
        zero = jnp.zeros_like(q)
        q_st = jnp.concatenate([jnp.where(lane < HEAD_DIM, q, zero),
                                jnp.where(lane >= HEAD_DIM, q, zero)], axis=0)
        for j in range(i + 1):
            kj = k_ref[0, 0, j * tk:(j + 1) * tk, :]
            s = lax.dot_general(q_st, kj, (((1,), (1,)), ((), ())),
                                preferred_element_type=F32)
            if j == i:
                s = jnp.where(causal, s, MASK_VALUE)
            s_ref[buf, :, j * tk:(j + 1) * tk] = s
            f = fold(s, jnp.maximum)
            m_ref[buf] = f if j == 0 else jnp.maximum(m_ref[buf], f)

    order = list(range(seq // tq - 1, -1, -1))
    for n in range(ATT_BUFS - 1):
        score_block(order[n])
    for n, i in enumerate(order):
        buf = i % ATT_BUFS
        if n + ATT_BUFS - 1 < len(order):
            score_block(order[n + ATT_BUFS - 1])
        m = jnp.max(m_ref[buf], axis=-1, keepdims=True)

        for j in range(i + 1):
            e = jnp.exp2(s_ref[buf, :, j * tk:(j + 1) * tk] - m)
            e_ref[buf, :, j * tk:(j + 1) * tk] = e.astype(BF16)

        kv = (i + 1) * tk
        acc = jnp.dot(e_ref[buf, :, 0:kv], v1_ref[0:kv, :], preferred_element_type=F32)
        p = acc[:, 0:V_HEAD_DIM] / acc[:, V_HEAD_DIM:2 * V_HEAD_DIM]
        o = p[0:tq] - lam * p[tq:rows]
        o = o * _rms_scale(o, 1e-6) * gs_ref[...]
        o_ref[0, i * tq:(i + 1) * tq, :] = (o * (1.0 - lambda_init)).astype(BF16)


def _attention(q, k, v, lam_vecs, g_subln, lambda_init):
    bsz, nh, seq, hd = q.shape
    head_spec = pl.BlockSpec((1, 1, seq, hd), lambda b, h: (b, h, 0, 0))
    return pl.pallas_call(
        functools.partial(_attn_kernel, lambda_init=lambda_init),
        grid=(bsz, nh),
        in_specs=[
            head_spec, head_spec, head_spec,
            _const_spec((4, HEAD_DIM)),
            _const_spec((1, V_HEAD_DIM)),
        ],
        out_specs=pl.BlockSpec((1, seq, hd), lambda b, h: (b, 0, h)),
        out_shape=jax.ShapeDtypeStruct((bsz, seq, nh * hd), BF16),
        scratch_shapes=[
            pltpu.VMEM((ATT_BUFS, 2 * ATT_TQ, seq), F32),
            pltpu.VMEM((ATT_BUFS, 2 * ATT_TQ, seq), BF16),
            pltpu.VMEM((ATT_BUFS, 2 * ATT_TQ, LANES), F32),
            pltpu.VMEM((seq, 2 * V_HEAD_DIM), BF16),
        ],
        compiler_params=_params(),
        name="diff_attn",
    )(q, k, v, lam_vecs, g_subln)


def _out_ffn_kernel(x_ref, o_ref, yc_ref, ga_ref, wa_ref, wo_ref, gf_ref,
                    wg_ref, wu_ref, wd_ref, gfin_ref, out_ref, *, ff_chunks):
    ya = jnp.dot(o_ref[...], wa_ref[...], preferred_element_type=F32)
    merged = yc_ref[...].astype(F32) + ga_ref[...].astype(F32) * ya
    x1 = x_ref[...] + jnp.dot(merged.astype(BF16), wo_ref[...], preferred_element_type=F32)
    row_scale = _rms_scale(x1, 1e-6)
    h2 = (x1 * gf_ref[...]).astype(BF16)
    acc = x1
    for c0, n in ff_chunks:
        g = jnp.dot(h2, wg_ref[:, c0:c0 + n], preferred_element_type=F32) * row_scale
        u = jnp.dot(h2, wu_ref[:, c0:c0 + n], preferred_element_type=F32) * row_scale
        f = (g * jax.nn.sigmoid(g) * u).astype(BF16)
        acc = acc + jnp.dot(f, wd_ref[c0:c0 + n, :], preferred_element_type=F32)
    out_ref[...] = acc * _rms_scale(acc, 1e-6) * gfin_ref[...]


def _out_ffn(x2d, o2d, yc2d, gates2d, w_attn, w_out, g_ffn, w_g, w_u, w_d, g_final):
    t, d = x2d.shape
    d_ff = w_g.shape[1]
    tm = ROW_TILE
    assert d_ff % MXU_COLS == 0
    half = (d_ff // MXU_COLS + 1) // 2 * MXU_COLS
    ff_chunks = ((0, half), (half, d_ff - half))
    row = lambda i: (i, 0)
    return pl.pallas_call(
        functools.partial(_out_ffn_kernel, ff_chunks=ff_chunks),
        grid=(t // tm,),
        in_specs=[
            pl.BlockSpec((tm, d), row),
            pl.BlockSpec((tm, d), row),
            pl.BlockSpec((tm, d), row),
            pl.BlockSpec((tm, d), lambda i: (i, 1)),
            _const_spec((d, d)),
            _const_spec((d, d)),
            _const_spec((1, d)),
            _const_spec((d, d_ff)),
            _const_spec((d, d_ff)),
            _const_spec((d_ff, d)),
            _const_spec((1, d)),
        ],
        out_specs=pl.BlockSpec((tm, d), row),
        out_shape=jax.ShapeDtypeStruct((t, d), F32),
        compiler_params=_params(),
        name="out_ffn",
    )(x2d, o2d, yc2d, gates2d, w_attn, w_out, g_ffn, w_g, w_u, w_d, g_final)


def _rope_tables(seq):
    inv_freq = ROPE_THETA ** (-jnp.arange(0, ROT_DIM, 2, dtype=F32) / ROT_DIM)
    ang = jnp.arange(seq, dtype=F32)[:, None] * inv_freq[None, :]
    lane = np.arange(LANES)
    in_comp = lane % HEAD_DIM
    freq = lane % (ROT_DIM // 2)
    cos = jnp.cos(ang)[:, freq]
    sin = jnp.sin(ang)[:, freq]
    first = jnp.asarray(in_comp < ROT_DIM // 2)[None, :]
    second = jnp.asarray((in_comp >= ROT_DIM // 2) & (in_comp < ROT_DIM))[None, :]
    cos_t = jnp.where(first | second, cos, 1.0)
    sin_a = jnp.where(first, -sin, 0.0)
    sin_b = jnp.where(second, sin, 0.0)
    base = jnp.stack([cos_t, sin_a, sin_b])
    return jnp.concatenate([base * (HEAD_DIM ** -0.5 * LOG2_E), base], axis=0)


def kernel(x, g_mix, w_in, b_gate, w_dw, b_dw, g_conv_ln, b_conv_ln, w_conv_pw,
           lambda_q1, lambda_k1, lambda_q2, lambda_k2, g_subln, w_attn_pw, w_out,
           g_ffn, w_ffn_gate, w_ffn_up, w_ffn_down, g_final):
    bsz, seq, d = x.shape
    assert g_mix.shape[0] == 1, "one layer only"
    l = 0
    lambda_init = 0.8 - 0.6 * math.exp(-0.3 * l)
    t = bsz * seq
    rope = _rope_tables(seq)
    u, q, k, v, gates = _in_proj(x, g_mix[l][None], w_in[l].astype(BF16),
                                 b_gate[l][None], rope)
    w_taps = jnp.broadcast_to(
        w_dw[l].astype(BF16).reshape(CONV_W, 1, d // LANES, LANES).transpose(2, 0, 1, 3),
        (d // LANES, CONV_W, PACKED_ROWS, LANES))
    yc = _conv_branch(u, w_taps, b_dw[l][None], g_conv_ln[l][None],
                      b_conv_ln[l][None], w_conv_pw[l].astype(BF16), gates)
    lam_vecs = jnp.stack([lambda_q1[l], lambda_k1[l], lambda_q2[l], lambda_k2[l]])
    o = _attention(q, k, v, lam_vecs, g_subln[l][None], lambda_init)
    out = _out_ffn(x.reshape(t, d), o.reshape(t, d), yc.reshape(t, d),
                   gates.reshape(t, 2 * d), w_attn_pw[l].astype(BF16),
                   w_out[l].astype(BF16), g_ffn[l][None],
                   w_ffn_gate[l].astype(BF16), w_ffn_up[l].astype(BF16),
                   w_ffn_down[l].astype(BF16), g_final[None])
    return out.reshape(bsz, seq, d)
```

```python
import functools
import math

import jax
import jax.numpy as jnp
import numpy as np
from jax import lax
from jax.experimental import pallas as pl
from jax.experimental.pallas import tpu as pltpu

F32 = jnp.float32
BF16 = jnp.bfloat16

N_HEADS = 8
HEAD_DIM = 64
V_HEAD_DIM = 2 * HEAD_DIM
ROT_DIM = HEAD_DIM // 4
ROPE_THETA = 500000.0
CONV_W = 31
LANES = 128
MXU_COLS = 256
VMEM_LIMIT_BYTES = 56 * 1024 * 1024

ROW_TILE = 512
COL_CHUNK = 256
ATT_TQ = 256
ATT_TK = 256
ATT_BUFS = 3
ATT_HEADS_PER_STEP = 2
SUBLANES = 8
CONV_TS = 256
CONV_HALO = 32
PACKED_ROWS = 16
CONV_PITCH = 20
CONV_GROUP = 8
MASK_VALUE = -1e30
LOG2_E = math.log2(math.e)


def _const_spec(shape):
    nd = len(shape)
    return pl.BlockSpec(shape, lambda *_: (0,) * nd, pipeline_mode=pl.Buffered(1))


def _params():
    return pltpu.CompilerParams(vmem_limit_bytes=VMEM_LIMIT_BYTES)


def _rms_scale(x, eps):
    return lax.rsqrt(jnp.mean(x * x, axis=-1, keepdims=True) + eps)


def _in_proj_kernel(x_ref, g_ref, w_ref, bg_ref, rope_ref,
                    u_ref, q_ref, k_ref, v_ref, gate_ref, *, d_model):
    x = x_ref[0]
    row_scale = _rms_scale(x, 1e-6)
    h = (x * g_ref[...]).astype(BF16)

    def proj(c0, n):
        return jnp.dot(h, w_ref[:, c0:c0 + n], preferred_element_type=F32) * row_scale

    c_glu = 0
    c_q = 2 * d_model
    c_k = c_q + d_model
    c_v = c_k + d_model
    c_gate = c_v + d_model

    for c in range(0, 2 * d_model, COL_CHUNK):
        z = proj(c_gate + c, COL_CHUNK) + bg_ref[:, c:c + COL_CHUNK]
        gate_ref[0, :, c:c + COL_CHUNK] = jax.nn.sigmoid(z).astype(BF16)

    for c in range(0, d_model, COL_CHUNK):
        a = proj(c_glu + c, COL_CHUNK)
        g = proj(c_glu + d_model + c, COL_CHUNK)
        u_ref[0, :, c:c + COL_CHUNK] = (a * jax.nn.sigmoid(g)).astype(BF16)

    for base, out_ref, t0 in ((c_q, q_ref, 0), (c_k, k_ref, 3)):
        cos_t = rope_ref[t0]
        sin_a = rope_ref[t0 + 1]
        sin_b = rope_ref[t0 + 2]
        for c in range(0, d_model, COL_CHUNK):
            z = proj(base + c, COL_CHUNK)
            for hh in range(COL_CHUNK // LANES):
                t = z[:, hh * LANES:(hh + 1) * LANES]
                r = (t * cos_t
                     + pltpu.roll(t, LANES - ROT_DIM // 2, 1) * sin_a
                     + pltpu.roll(t, ROT_DIM // 2, 1) * sin_b)
                out_ref[0, c // LANES + hh] = r.astype(BF16)

    for c in range(0, d_model, COL_CHUNK):
        z = proj(c_v + c, COL_CHUNK)
        for hh in range(COL_CHUNK // LANES):
            v_ref[0, c // LANES + hh] = z[:, hh * LANES:(hh + 1) * LANES].astype(BF16)


def _in_proj(x, g_mix, w_in, b_gate, rope):
    bsz, seq, d = x.shape
    n_cols = w_in.shape[1]
    tm = ROW_TILE
    grid = (bsz, seq // tm)
    head_shape = jax.ShapeDtypeStruct((bsz, N_HEADS, seq, V_HEAD_DIM), BF16)
    head_spec = pl.BlockSpec((1, N_HEADS, tm, V_HEAD_DIM), lambda b, i: (b, 0, i, 0))
    return pl.pallas_call(
        functools.partial(_in_proj_kernel, d_model=d),
        grid=grid,
        in_specs=[
            pl.BlockSpec((1, tm, d), lambda b, i: (b, i, 0)),
            _const_spec((1, d)),
            _const_spec((d, n_cols)),
            _const_spec((1, 2 * d)),
            pl.BlockSpec((6, tm, LANES), lambda b, i: (0, i, 0)),
        ],
        out_specs=[
            pl.BlockSpec((1, tm, d), lambda b, i: (b, i, 0)),
            head_spec, head_spec, head_spec,
            pl.BlockSpec((1, tm, 2 * d), lambda b, i: (b, i, 0)),
        ],
        out_shape=[
            jax.ShapeDtypeStruct((bsz, seq, d), BF16),
            head_shape, head_shape, head_shape,
            jax.ShapeDtypeStruct((bsz, seq, 2 * d), BF16),
        ],
        compiler_params=_params(),
        name="in_proj",
    )(x, g_mix, w_in, b_gate, rope)


def _conv_kernel(u_ref, wdw_ref, bdw_ref, y_ref, tail_ref, etail_ref, seg_ref, even_ref, odd_ref):
    ts = CONV_TS
    n_ch = tail_ref.shape[1]
    n_slabs = n_ch // LANES
    halo_words = CONV_HALO // 2
    copy_base = SUBLANES * CONV_PITCH

    @pl.when(pl.program_id(1) == 0)
    def _():
        tail_ref[...] = jnp.zeros((halo_words, n_ch), jnp.uint32)
        etail_ref[...] = jnp.zeros((SUBLANES, n_ch), F32)

    words = pltpu.bitcast(u_ref[0], jnp.uint32)
    window = [tail_ref[SUBLANES * v:SUBLANES * (v + 1), :] for v in range(halo_words // SUBLANES)]
    window += [words[SUBLANES * v:SUBLANES * (v + 1), :] for v in range(ts // PACKED_ROWS)]
    for v, rows in enumerate(window):
        for c in range(n_slabs):
            piece = rows[:, c * LANES:(c + 1) * LANES]
            seg_ref[c, pl.ds(v, SUBLANES, stride=CONV_PITCH), :] = piece
            if v > 0:
                seg_ref[c, pl.ds(copy_base + v - 1, SUBLANES, stride=CONV_PITCH), :] = piece
    tail_ref[...] = words[ts // 2 - halo_words:ts // 2, :]

    lead = CONV_HALO - (CONV_W - 1)
    for c in range(n_slabs):
        lanes = slice(c * LANES, (c + 1) * LANES)

        def group(n, carry, c=c, lanes=lanes):
            a0 = n * CONV_GROUP
            w = [wdw_ref[c, j] for j in range(CONV_W)]
            loaded = {}
            evens, odds = [], []
            for g in range(CONV_GROUP):
                acc = [None, None]
                for j in range(CONV_W):
                    parity = (lead + j) % 2
                    start = lead + j + parity
                    word = start // 2 + g * SUBLANES
                    if word not in loaded:
                        q, r = divmod(word, SUBLANES)
                        raw = seg_ref[c, pl.ds(r * CONV_PITCH + a0 + q, SUBLANES, stride=CONV_PITCH), :]
                        loaded[word] = pltpu.bitcast(raw, BF16)
                    term = loaded[word].astype(F32) * w[j].astype(F32)
                    acc[parity] = term if acc[parity] is None else acc[parity] + term
                evens.append(acc[0])
                odds.append(acc[1])
            base = pl.multiple_of(a0 * PACKED_ROWS, CONV_GROUP * PACKED_ROWS)
            even_ref[pl.ds(base, CONV_GROUP * PACKED_ROWS), lanes] = jnp.concatenate(evens, axis=0)
            odd_ref[pl.ds(base, CONV_GROUP * PACKED_ROWS), lanes] = jnp.concatenate(odds, axis=0)
            return carry

        lax.fori_loop(0, ts // PACKED_ROWS // CONV_GROUP, group, 0)

    odd = odd_ref[...]
    prev_last = jnp.broadcast_to(etail_ref[SUBLANES - 1:SUBLANES, :], (ts, n_ch))
    first_row = lax.broadcasted_iota(jnp.int32, (ts, n_ch), 0) == 0
    shifted = jnp.where(first_row, prev_last, pltpu.roll(odd, 1, 0))
    y_ref[0] = even_ref[...] + shifted + bdw_ref[...]
    etail_ref[...] = odd[ts - SUBLANES:ts, :]


def _conv_taps(u, w_dw, b_dw):
    bsz, seq, c = u.shape
    ts = CONV_TS
    return pl.pallas_call(
        _conv_kernel,
        grid=(bsz, seq // ts),
        in_specs=[
            pl.BlockSpec((1, ts, c), lambda b, s: (b, s, 0)),
            _const_spec((c // LANES, CONV_W, PACKED_ROWS, LANES)),
            _const_spec((1, c)),
        ],
        out_specs=pl.BlockSpec((1, ts, c), lambda b, s: (b, s, 0)),
        out_shape=jax.ShapeDtypeStruct((bsz, seq, c), F32),
        scratch_shapes=[
            pltpu.VMEM((CONV_HALO // 2, c), jnp.uint32),
            pltpu.VMEM((SUBLANES, c), F32),
            pltpu.VMEM((c // LANES, 2 * SUBLANES * CONV_PITCH, LANES), jnp.uint32),
            pltpu.VMEM((ts, c), F32),
            pltpu.VMEM((ts, c), F32),
        ],
        compiler_params=pltpu.CompilerParams(
            vmem_limit_bytes=VMEM_LIMIT_BYTES,
            dimension_semantics=("arbitrary", "arbitrary")),
        name="conv_taps",
    )(u, w_dw, b_dw)


def _attn_kernel(q_ref, k_ref, v_ref, lam_ref, gs_ref, o_ref, s_ref, e_ref, m_ref, v1_ref,
                 *, lambda_init):
    tq, tk = ATT_TQ, ATT_TK
    rows = 2 * tq
    seq = q_ref.shape[2]

    lam = (jnp.exp(jnp.sum(lam_ref[0:1, :] * lam_ref[1:2, :], axis=-1, keepdims=True))
           - jnp.exp(jnp.sum(lam_ref[2:3, :] * lam_ref[3:4, :], axis=-1, keepdims=True))
           + lambda_init)

    r_idx = lax.broadcasted_iota(jnp.int32, (rows, tk), 0)
    c_idx = lax.broadcasted_iota(jnp.int32, (rows, tk), 1)
    causal = c_idx <= jnp.where(r_idx >= tq, r_idx - tq, r_idx)
    lane = lax.broadcasted_iota(jnp.int32, (tq, V_HEAD_DIM), 1)

    n_heads = q_ref.shape[1]
    for hh in range(n_heads):
        v1_ref[hh, :, 0:V_HEAD_DIM] = v_ref[0, hh]
        v1_ref[hh, :, V_HEAD_DIM:2 * V_HEAD_DIM] = jnp.ones((seq, V_HEAD_DIM), BF16)

    def fold(x, op):
        out = x[:, 0:LANES]
        for c in range(LANES, tk, LANES):
            out = op(out, x[:, c:c + LANES])
        return out

    order = [(hh, i) for hh in range(n_heads) for i in range(seq // tq - 1, -1, -1)]

    def score_block(n):
        hh, i = order[n]
        buf = n % ATT_BUFS
        q = q_ref[0, hh, i * tq:(i + 1) * tq, :]
        zero = jnp.zeros_like(q)
        q_st = jnp.concatenate([jnp.where(lane < HEAD_DIM, q, zero),
                                jnp.where(lane >= HEAD_DIM, q, zero)], axis=0)
        for j in range(i + 1):
            kj = k_ref[0, hh, j * tk:(j + 1) * tk, :]
            s = lax.dot_general(q_st, kj, (((1,), (1,)), ((), ())),
                                preferred_element_type=F32)
            if j == i:
                s = jnp.where(causal, s, MASK_VALUE)
            s_ref[buf, :, j * tk:(j + 1) * tk] = s
            f = fold(s, jnp.maximum)
            m_ref[buf] = f if j == 0 else jnp.maximum(m_ref[buf], f)

    for n in range(ATT_BUFS - 1):
        score_block(n)
    for n, (hh, i) in enumerate(order):
        buf = n % ATT_BUFS
        if n + ATT_BUFS - 1 < len(order):
            score_block(n + ATT_BUFS - 1)
        m = jnp.max(m_ref[buf], axis=-1, keepdims=True)

        for j in range(i + 1):
            e = jnp.exp2(s_ref[buf, :, j * tk:(j + 1) * tk] - m)
            e_ref[buf, :, j * tk:(j + 1) * tk] = e.astype(BF16)

        kv = (i + 1) * tk
        acc = jnp.dot(e_ref[buf, :, 0:kv], v1_ref[hh, 0:kv, :], preferred_element_type=F32)
        p = acc[:, 0:V_HEAD_DIM] / acc[:, V_HEAD_DIM:2 * V_HEAD_DIM]
        o = p[0:tq] - lam * p[tq:rows]
        o = o * _rms_scale(o, 1e-6) * gs_ref[...]
        o_ref[0, i * tq:(i + 1) * tq, hh * V_HEAD_DIM:(hh + 1) * V_HEAD_DIM] = (
            (o * (1.0 - lambda_init)).astype(BF16))


def _attention(q, k, v, lam_vecs, g_subln, lambda_init):
    bsz, nh, seq, hd = q.shape
    hb = ATT_HEADS_PER_STEP
    head_spec = pl.BlockSpec((1, hb, seq, hd), lambda b, h: (b, h, 0, 0))
    return pl.pallas_call(
        functools.partial(_attn_kernel, lambda_init=lambda_init),
        grid=(bsz, nh // hb),
        in_specs=[
            head_spec, head_spec, head_spec,
            _const_spec((4, HEAD_DIM)),
            _const_spec((1, V_HEAD_DIM)),
        ],
        out_specs=pl.BlockSpec((1, seq, hb * hd), lambda b, h: (b, 0, h)),
        out_shape=jax.ShapeDtypeStruct((bsz, seq, nh * hd), BF16),
        scratch_shapes=[
            pltpu.VMEM((ATT_BUFS, 2 * ATT_TQ, seq), F32),
            pltpu.VMEM((ATT_BUFS, 2 * ATT_TQ, seq), BF16),
            pltpu.VMEM((ATT_BUFS, 2 * ATT_TQ, LANES), F32),
            pltpu.VMEM((hb, seq, 2 * V_HEAD_DIM), BF16),
        ],
        compiler_params=_params(),
        name="diff_attn",
    )(q, k, v, lam_vecs, g_subln)


def _out_ffn_kernel(x_ref, o_ref, cv_ref, gate_ref, gln_ref, bln_ref, wpw_ref, wa_ref, wo_ref,
                    gf_ref, wg_ref, wu_ref, wd_ref, gfin_ref, out_ref, *, ff_chunks):
    d = x_ref.shape[1]
    ya = jnp.dot(o_ref[...], wa_ref[...], preferred_element_type=F32)
    cv = cv_ref[...]
    mu = jnp.mean(cv, axis=-1, keepdims=True)
    xc = cv - mu
    var = jnp.mean(xc * xc, axis=-1, keepdims=True)
    y = xc * lax.rsqrt(var + 1e-5) * gln_ref[...] + bln_ref[...]
    y = (y * jax.nn.sigmoid(y)).astype(BF16)
    yc = jnp.dot(y, wpw_ref[...], preferred_element_type=F32)
    merged = gate_ref[:, 0:d].astype(F32) * yc + gate_ref[:, d:2 * d].astype(F32) * ya
    x1 = x_ref[...] + jnp.dot(merged.astype(BF16), wo_ref[...], preferred_element_type=F32)
    row_scale = _rms_scale(x1, 1e-6)
    h2 = (x1 * gf_ref[...]).astype(BF16)
    acc = x1
    for c0, n in ff_chunks:
        g = jnp.dot(h2, wg_ref[:, c0:c0 + n], preferred_element_type=F32) * row_scale
        u = jnp.dot(h2, wu_ref[:, c0:c0 + n], preferred_element_type=F32) * row_scale
        f = (g * jax.nn.sigmoid(g) * u).astype(BF16)
        acc = acc + jnp.dot(f, wd_ref[c0:c0 + n, :], preferred_element_type=F32)
    out_ref[...] = acc * _rms_scale(acc, 1e-6) * gfin_ref[...]


def _out_ffn(x2d, o2d, cv2d, gates2d, g_ln, b_ln, w_pw, w_attn, w_out, g_ffn, w_g, w_u, w_d,
             g_final):
    t, d = x2d.shape
    d_ff = w_g.shape[1]
    tm = ROW_TILE
    assert d_ff % MXU_COLS == 0
    half = (d_ff // MXU_COLS + 1) // 2 * MXU_COLS
    ff_chunks = ((0, half), (half, d_ff - half))
    row = lambda i: (i, 0)
    return pl.pallas_call(
        functools.partial(_out_ffn_kernel, ff_chunks=ff_chunks),
        grid=(t // tm,),
        in_specs=[
            pl.BlockSpec((tm, d), row),
            pl.BlockSpec((tm, d), row),
            pl.BlockSpec((tm, d), row),
            pl.BlockSpec((tm, 2 * d), row),
            _const_spec((1, d)),
            _const_spec((1, d)),
            _const_spec((d, d)),
            _const_spec((d, d)),
            _const_spec((d, d)),
            _const_spec((1, d)),
            _const_spec((d, d_ff)),
            _const_spec((d, d_ff)),
            _const_spec((d_ff, d)),
            _const_spec((1, d)),
        ],
        out_specs=pl.BlockSpec((tm, d), row),
        out_shape=jax.ShapeDtypeStruct((t, d), F32),
        compiler_params=_params(),
        name="out_ffn",
    )(x2d, o2d, cv2d, gates2d, g_ln, b_ln, w_pw, w_attn, w_out, g_ffn, w_g, w_u, w_d, g_final)


def _rope_tables(seq):
    inv_freq = ROPE_THETA ** (-jnp.arange(0, ROT_DIM, 2, dtype=F32) / ROT_DIM)
    ang = jnp.arange(seq, dtype=F32)[:, None] * inv_freq[None, :]
    lane = np.arange(LANES)
    in_comp = lane % HEAD_DIM
    freq = lane % (ROT_DIM // 2)
    cos = jnp.cos(ang)[:, freq]
    sin = jnp.sin(ang)[:, freq]
    first = jnp.asarray(in_comp < ROT_DIM // 2)[None, :]
    second = jnp.asarray((in_comp >= ROT_DIM // 2) & (in_comp < ROT_DIM))[None, :]
    cos_t = jnp.where(first | second, cos, 1.0)
    sin_a = jnp.where(first, -sin, 0.0)
    sin_b = jnp.where(second, sin, 0.0)
    base = jnp.stack([cos_t, sin_a, sin_b])
    return jnp.concatenate([base * (HEAD_DIM ** -0.5 * LOG2_E), base], axis=0)


def kernel(x, g_mix, w_in, b_gate, w_dw, b_dw, g_conv_ln, b_conv_ln, w_conv_pw,
           lambda_q1, lambda_k1, lambda_q2, lambda_k2, g_subln, w_attn_pw, w_out,
           g_ffn, w_ffn_gate, w_ffn_up, w_ffn_down, g_final):
    bsz, seq, d = x.shape
    assert g_mix.shape[0] == 1, "one layer only"
    l = 0
    lambda_init = 0.8 - 0.6 * math.exp(-0.3 * l)
    t = bsz * seq
    rope = _rope_tables(seq)
    u, q, k, v, gates = _in_proj(x, g_mix[l][None], w_in[l].astype(BF16),
                                 b_gate[l][None], rope)
    w_taps = jnp.broadcast_to(
        w_dw[l].astype(BF16).reshape(CONV_W, 1, d // LANES, LANES).transpose(2, 0, 1, 3),
        (d // LANES, CONV_W, PACKED_ROWS, LANES))
    cv = _conv_taps(u, w_taps, b_dw[l][None])
    lam_vecs = jnp.stack([lambda_q1[l], lambda_k1[l], lambda_q2[l], lambda_k2[l]])
    o = _attention(q, k, v, lam_vecs, g_subln[l][None], lambda_init)
    out = _out_ffn(x.reshape(t, d), o.reshape(t, d), cv.reshape(t, d),
                   gates.reshape(t, 2 * d), g_conv_ln[l][None], b_conv_ln[l][None],
                   w_conv_pw[l].astype(BF16), w_attn_pw[l].astype(BF16),
                   w_out[l].astype(BF16), g_ffn[l][None],
                   w_ffn_gate[l].astype(BF16), w_ffn_up[l].astype(BF16),
                   w_ffn_down[l].astype(BF16), g_final[None])
    return out.reshape(bsz, seq, d)
```

```python
import functools
import math

import jax
import jax.numpy as jnp
import numpy as np
from jax import lax
from jax.experimental import pallas as pl
from jax.experimental.pallas import tpu as pltpu

F32 = jnp.float32
BF16 = jnp.bfloat16

N_HEADS = 8
HEAD_DIM = 64
V_HEAD_DIM = 2 * HEAD_DIM
ROT_DIM = HEAD_DIM // 4
ROPE_THETA = 500000.0
CONV_W = 31
LANES = 128
MXU_COLS = 256
VMEM_LIMIT_BYTES = 56 * 1024 * 1024

ROW_TILE = 512
COL_CHUNK = 256
ATT_TQ = 256
ATT_TK = 256
ATT_BUFS = 3
ATT_HEADS_PER_STEP = 2
SUBLANES = 8
CONV_TS = 256
CONV_HALO = 32
PACKED_ROWS = 16
CONV_PITCH = 20
CONV_GROUP = 4
MASK_VALUE = -1e30
LOG2_E = math.log2(math.e)


def _const_spec(shape):
    nd = len(shape)
    return pl.BlockSpec(shape, lambda *_: (0,) * nd, pipeline_mode=pl.Buffered(1))


def _params():
    return pltpu.CompilerParams(vmem_limit_bytes=VMEM_LIMIT_BYTES)


def _rms_scale(x, eps):
    return lax.rsqrt(jnp.mean(x * x, axis=-1, keepdims=True) + eps)


def _in_proj_kernel(x_ref, g_ref, w_ref, bg_ref, rope_ref,
                    u_ref, q_ref, k_ref, v_ref, gate_ref, *, d_model):
    x = x_ref[0]
    row_scale = _rms_scale(x, 1e-6)
    h = (x * g_ref[...]).astype(BF16)

    def proj(c0, n):
        return jnp.dot(h, w_ref[:, c0:c0 + n], preferred_element_type=F32) * row_scale

    c_glu = 0
    c_q = 2 * d_model
    c_k = c_q + d_model
    c_v = c_k + d_model
    c_gate = c_v + d_model

    for c in range(0, 2 * d_model, COL_CHUNK):
        z = proj(c_gate + c, COL_CHUNK) + bg_ref[:, c:c + COL_CHUNK]
        gate_ref[0, :, c:c + COL_CHUNK] = jax.nn.sigmoid(z).astype(BF16)

    for c in range(0, d_model, COL_CHUNK):
        a = proj(c_glu + c, COL_CHUNK)
        g = proj(c_glu + d_model + c, COL_CHUNK)
        u_ref[0, :, c:c + COL_CHUNK] = (a * jax.nn.sigmoid(g)).astype(BF16)

    for base, out_ref, t0 in ((c_q, q_ref, 0), (c_k, k_ref, 3)):
        cos_t = rope_ref[t0]
        sin_a = rope_ref[t0 + 1]
        sin_b = rope_ref[t0 + 2]
        for c in range(0, d_model, COL_CHUNK):
            z = proj(base + c, COL_CHUNK)
            for hh in range(COL_CHUNK // LANES):
                t = z[:, hh * LANES:(hh + 1) * LANES]
                r = (t * cos_t
                     + pltpu.roll(t, LANES - ROT_DIM // 2, 1) * sin_a
                     + pltpu.roll(t, ROT_DIM // 2, 1) * sin_b)
                out_ref[0, c // LANES + hh] = r.astype(BF16)

    for c in range(0, d_model, COL_CHUNK):
        z = proj(c_v + c, COL_CHUNK)
        for hh in range(COL_CHUNK // LANES):
            v_ref[0, c // LANES + hh] = z[:, hh * LANES:(hh + 1) * LANES].astype(BF16)


def _in_proj(x, g_mix, w_in, b_gate, rope):
    bsz, seq, d = x.shape
    n_cols = w_in.shape[1]
    tm = ROW_TILE
    grid = (bsz, seq // tm)
    head_shape = jax.ShapeDtypeStruct((bsz, N_HEADS, seq, V_HEAD_DIM), BF16)
    head_spec = pl.BlockSpec((1, N_HEADS, tm, V_HEAD_DIM), lambda b, i: (b, 0, i, 0))
    return pl.pallas_call(
        functools.partial(_in_proj_kernel, d_model=d),
        grid=grid,
        in_specs=[
            pl.BlockSpec((1, tm, d), lambda b, i: (b, i, 0)),
            _const_spec((1, d)),
            _const_spec((d, n_cols)),
            _const_spec((1, 2 * d)),
            pl.BlockSpec((6, tm, LANES), lambda b, i: (0, i, 0)),
        ],
        out_specs=[
            pl.BlockSpec((1, tm, d), lambda b, i: (b, i, 0)),
            head_spec, head_spec, head_spec,
            pl.BlockSpec((1, tm, 2 * d), lambda b, i: (b, i, 0)),
        ],
        out_shape=[
            jax.ShapeDtypeStruct((bsz, seq, d), BF16),
            head_shape, head_shape, head_shape,
            jax.ShapeDtypeStruct((bsz, seq, 2 * d), BF16),
        ],
        compiler_params=_params(),
        name="in_proj",
    )(x, g_mix, w_in, b_gate, rope)


def _conv_kernel(u_ref, wdw_ref, bdw_ref, y_ref, tail_ref, etail_ref, seg_ref, even_ref, odd_ref):
    ts = CONV_TS
    n_ch = tail_ref.shape[1]
    n_slabs = n_ch // LANES
    halo_words = CONV_HALO // 2
    copy_base = SUBLANES * CONV_PITCH

    @pl.when(pl.program_id(1) == 0)
    def _():
        tail_ref[...] = jnp.zeros((halo_words, n_ch), jnp.uint32)
        etail_ref[...] = jnp.zeros((SUBLANES, n_ch), F32)

    words = pltpu.bitcast(u_ref[0], jnp.uint32)
    window = [tail_ref[SUBLANES * v:SUBLANES * (v + 1), :] for v in range(halo_words // SUBLANES)]
    window += [words[SUBLANES * v:SUBLANES * (v + 1), :] for v in range(ts // PACKED_ROWS)]
    for v, rows in enumerate(window):
        for c in range(n_slabs):
            piece = rows[:, c * LANES:(c + 1) * LANES]
            seg_ref[c, pl.ds(v, SUBLANES, stride=CONV_PITCH), :] = piece
            if v > 0:
                seg_ref[c, pl.ds(copy_base + v - 1, SUBLANES, stride=CONV_PITCH), :] = piece
    tail_ref[...] = words[ts // 2 - halo_words:ts // 2, :]

    lead = CONV_HALO - (CONV_W - 1)
    for c in range(n_slabs):
        lanes = slice(c * LANES, (c + 1) * LANES)

        def group(n, carry, c=c, lanes=lanes):
            a0 = n * CONV_GROUP
            w = [wdw_ref[c, j] for j in range(CONV_W)]
            loaded = {}
            evens, odds = [], []
            for g in range(CONV_GROUP):
                acc = [None, None]
                for j in range(CONV_W):
                    parity = (lead + j) % 2
                    start = lead + j + parity
                    word = start // 2 + g * SUBLANES
                    if word not in loaded:
                        q, r = divmod(word, SUBLANES)
                        raw = seg_ref[c, pl.ds(r * CONV_PITCH + a0 + q, SUBLANES, stride=CONV_PITCH), :]
                        loaded[word] = pltpu.bitcast(raw, BF16)
                    term = loaded[word].astype(F32) * w[j].astype(F32)
                    acc[parity] = term if acc[parity] is None else acc[parity] + term
                evens.append(acc[0])
                odds.append(acc[1])
            base = pl.multiple_of(a0 * PACKED_ROWS, CONV_GROUP * PACKED_ROWS)
            even_ref[pl.ds(base, CONV_GROUP * PACKED_ROWS), lanes] = jnp.concatenate(evens, axis=0)
            odd_ref[pl.ds(base, CONV_GROUP * PACKED_ROWS), lanes] = jnp.concatenate(odds, axis=0)
            return carry

        lax.fori_loop(0, ts // PACKED_ROWS // CONV_GROUP, group, 0)

    odd = odd_ref[...]
    prev_last = jnp.broadcast_to(etail_ref[SUBLANES - 1:SUBLANES, :], (ts, n_ch))
    first_row = lax.broadcasted_iota(jnp.int32, (ts, n_ch), 0) == 0
    shifted = jnp.where(first_row, prev_last, pltpu.roll(odd, 1, 0))
    y_ref[0] = even_ref[...] + shifted + bdw_ref[...]
    etail_ref[...] = odd[ts - SUBLANES:ts, :]


def _conv_taps(u, w_dw, b_dw):
    bsz, seq, c = u.shape
    ts = CONV_TS
    return pl.pallas_call(
        _conv_kernel,
        grid=(bsz, seq // ts),
        in_specs=[
            pl.BlockSpec((1, ts, c), lambda b, s: (b, s, 0)),
            _const_spec((c // LANES, CONV_W, PACKED_ROWS, LANES)),
            _const_spec((1, c)),
        ],
        out_specs=pl.BlockSpec((1, ts, c), lambda b, s: (b, s, 0)),
        out_shape=jax.ShapeDtypeStruct((bsz, seq, c), F32),
        scratch_shapes=[
            pltpu.VMEM((CONV_HALO // 2, c), jnp.uint32),
            pltpu.VMEM((SUBLANES, c), F32),
            pltpu.VMEM((c // LANES, 2 * SUBLANES * CONV_PITCH, LANES), jnp.uint32),
            pltpu.VMEM((ts, c), F32),
            pltpu.VMEM((ts, c), F32),
        ],
        compiler_params=pltpu.CompilerParams(
            vmem_limit_bytes=VMEM_LIMIT_BYTES,
            dimension_semantics=("arbitrary", "arbitrary")),
        name="conv_taps",
    )(u, w_dw, b_dw)


def _attn_kernel(q_ref, k_ref, v_ref, lam_ref, gs_ref, o_ref, s_ref, e_ref, m_ref, v1_ref,
                 *, lambda_init):
    tq, tk = ATT_TQ, ATT_TK
    rows = 2 * tq
    seq = q_ref.shape[2]

    lam = (jnp.exp(jnp.sum(lam_ref[0:1, :] * lam_ref[1:2, :], axis=-1, keepdims=True))
           - jnp.exp(jnp.sum(lam_ref[2:3, :] * lam_ref[3:4, :], axis=-1, keepdims=True))
           + lambda_init)

    r_idx = lax.broadcasted_iota(jnp.int32, (rows, tk), 0)
    c_idx = lax.broadcasted_iota(jnp.int32, (rows, tk), 1)
    causal = c_idx <= jnp.where(r_idx >= tq, r_idx - tq, r_idx)
    lane = lax.broadcasted_iota(jnp.int32, (tq, V_HEAD_DIM), 1)

    n_heads = q_ref.shape[1]
    for hh in range(n_heads):
        v1_ref[hh, :, 0:V_HEAD_DIM] = v_ref[0, hh]
        v1_ref[hh, :, V_HEAD_DIM:2 * V_HEAD_DIM] = jnp.ones((seq, V_HEAD_DIM), BF16)

    def fold(x, op):
        out = x[:, 0:LANES]
        for c in range(LANES, tk, LANES):
            out = op(out, x[:, c:c + LANES])
        return out

    order = [(hh, i) for hh in range(n_heads) for i in range(seq // tq - 1, -1, -1)]

    def score_block(n):
        hh, i = order[n]
        buf = n % ATT_BUFS
        q = q_ref[0, hh, i * tq:(i + 1) * tq, :]
        zero = jnp.zeros_like(q)
        q_st = jnp.concatenate([jnp.where(lane < HEAD_DIM, q, zero),
                                jnp.where(lane >= HEAD_DIM, q, zero)], axis=0)
        for j in range(i + 1):
            kj = k_ref[0, hh, j * tk:(j + 1) * tk, :]
            s = lax.dot_general(q_st, kj, (((1,), (1,)), ((), ())),
                                preferred_element_type=F32)
            if j == i:
                s = jnp.where(causal, s, MASK_VALUE)
            s_ref[buf, :, j * tk:(j + 1) * tk] = s
            f = fold(s, jnp.maximum)
            m_ref[buf] = f if j == 0 else jnp.maximum(m_ref[buf], f)

    for n in range(ATT_BUFS - 1):
        score_block(n)
    for n, (hh, i) in enumerate(order):
        buf = n % ATT_BUFS
        if n + ATT_BUFS - 1 < len(order):
            score_block(n + ATT_BUFS - 1)
        m = jnp.max(m_ref[buf], axis=-1, keepdims=True)

        for j in range(i + 1):
            e = jnp.exp2(s_ref[buf, :, j * tk:(j + 1) * tk] - m)
            e_ref[buf, :, j * tk:(j + 1) * tk] = e.astype(BF16)

        kv = (i + 1) * tk
        acc = jnp.dot(e_ref[buf, :, 0:kv], v1_ref[hh, 0:kv, :], preferred_element_type=F32)
        p = acc[:, 0:V_HEAD_DIM] / acc[:, V_HEAD_DIM:2 * V_HEAD_DIM]
        o = p[0:tq] - lam * p[tq:rows]
        o = o * _rms_scale(o, 1e-6) * gs_ref[...]
        o_ref[0, i * tq:(i + 1) * tq, hh * V_HEAD_DIM:(hh + 1) * V_HEAD_DIM] = (
            (o * (1.0 - lambda_init)).astype(BF16))


def _attention(q, k, v, lam_vecs, g_subln, lambda_init):
    bsz, nh, seq, hd = q.shape
    hb = ATT_HEADS_PER_STEP
    head_spec = pl.BlockSpec((1, hb, seq, hd), lambda b, h: (b, h, 0, 0))
    return pl.pallas_call(
        functools.partial(_attn_kernel, lambda_init=lambda_init),
        grid=(bsz, nh // hb),
        in_specs=[
            head_spec, head_spec, head_spec,
            _const_spec((4, HEAD_DIM)),
            _const_spec((1, V_HEAD_DIM)),
        ],
        out_specs=pl.BlockSpec((1, seq, hb * hd), lambda b, h: (b, 0, h)),
        out_shape=jax.ShapeDtypeStruct((bsz, seq, nh * hd), BF16),
        scratch_shapes=[
            pltpu.VMEM((ATT_BUFS, 2 * ATT_TQ, seq), F32),
            pltpu.VMEM((ATT_BUFS, 2 * ATT_TQ, seq), BF16),
            pltpu.VMEM((ATT_BUFS, 2 * ATT_TQ, LANES), F32),
            pltpu.VMEM((hb, seq, 2 * V_HEAD_DIM), BF16),
        ],
        compiler_params=_params(),
        name="diff_attn",
    )(q, k, v, lam_vecs, g_subln)


def _out_ffn_kernel(x_ref, o_ref, cv_ref, gate_ref, gln_ref, bln_ref, wpw_ref, wa_ref, wo_ref,
                    gf_ref, wg_ref, wu_ref, wd_ref, gfin_ref, out_ref, *, ff_chunks):
    d = x_ref.shape[1]
    ya = jnp.dot(o_ref[...], wa_ref[...], preferred_element_type=F32)
    cv = cv_ref[...]
    mu = jnp.mean(cv, axis=-1, keepdims=True)
    xc = cv - mu
    var = jnp.mean(xc * xc, axis=-1, keepdims=True)
    y = xc * lax.rsqrt(var + 1e-5) * gln_ref[...] + bln_ref[...]
    y = (y * jax.nn.sigmoid(y)).astype(BF16)
    yc = jnp.dot(y, wpw_ref[...], preferred_element_type=F32)
    merged = gate_ref[:, 0:d].astype(F32) * yc + gate_ref[:, d:2 * d].astype(F32) * ya
    x1 = x_ref[...] + jnp.dot(merged.astype(BF16), wo_ref[...], preferred_element_type=F32)
    row_scale = _rms_scale(x1, 1e-6)
    h2 = (x1 * gf_ref[...]).astype(BF16)
    acc = x1
    for c0, n in ff_chunks:
        g = jnp.dot(h2, wg_ref[:, c0:c0 + n], preferred_element_type=F32) * row_scale
        u = jnp.dot(h2, wu_ref[:, c0:c0 + n], preferred_element_type=F32) * row_scale
        f = (g * jax.nn.sigmoid(g) * u).astype(BF16)
        acc = acc + jnp.dot(f, wd_ref[c0:c0 + n, :], preferred_element_type=F32)
    out_ref[...] = acc * _rms_scale(acc, 1e-6) * gfin_ref[...]


def _out_ffn(x2d, o2d, cv2d, gates2d, g_ln, b_ln, w_pw, w_attn, w_out, g_ffn, w_g, w_u, w_d,
             g_final):
    t, d = x2d.shape
    d_ff = w_g.shape[1]
    tm = ROW_TILE
    assert d_ff % MXU_COLS == 0
    half = (d_ff // MXU_COLS + 1) // 2 * MXU_COLS
    ff_chunks = ((0, half), (half, d_ff - half))
    row = lambda i: (i, 0)
    return pl.pallas_call(
        functools.partial(_out_ffn_kernel, ff_chunks=ff_chunks),
        grid=(t // tm,),
        in_specs=[
            pl.BlockSpec((tm, d), row),
            pl.BlockSpec((tm, d), row),
            pl.BlockSpec((tm, d), row),
            pl.BlockSpec((tm, 2 * d), row),
            _const_spec((1, d)),
            _const_spec((1, d)),
            _const_spec((d, d)),
            _const_spec((d, d)),
            _const_spec((d, d)),
            _const_spec((1, d)),
            _const_spec((d, d_ff)),
            _const_spec((d, d_ff)),
            _const_spec((d_ff, d)),
            _const_spec((1, d)),
        ],
        out_specs=pl.BlockSpec((tm, d), row),
        out_shape=jax.ShapeDtypeStruct((t, d), F32),
        compiler_params=_params(),
        name="out_ffn",
    )(x2d, o2d, cv2d, gates2d, g_ln, b_ln, w_pw, w_attn, w_out, g_ffn, w_g, w_u, w_d, g_final)


def _rope_tables(seq):
    inv_freq = ROPE_THETA ** (-jnp.arange(0, ROT_DIM, 2, dtype=F32) / ROT_DIM)
    ang = jnp.arange(seq, dtype=F32)[:, None] * inv_freq[None, :]
    lane = np.arange(LANES)
    in_comp = lane % HEAD_DIM
    freq = lane % (ROT_DIM // 2)
    cos = jnp.cos(ang)[:, freq]
    sin = jnp.sin(ang)[:, freq]
    first = jnp.asarray(in_comp < ROT_DIM // 2)[None, :]
    second = jnp.asarray((in_comp >= ROT_DIM // 2) & (in_comp < ROT_DIM))[None, :]
    cos_t = jnp.where(first | second, cos, 1.0)
    sin_a = jnp.where(first, -sin, 0.0)
    sin_b = jnp.where(second, sin, 0.0)
    base = jnp.stack([cos_t, sin_a, sin_b])
    return jnp.concatenate([base * (HEAD_DIM ** -0.5 * LOG2_E), base], axis=0)


def kernel(x, g_mix, w_in, b_gate, w_dw, b_dw, g_conv_ln, b_conv_ln, w_conv_pw,
           lambda_q1, lambda_k1, lambda_q2, lambda_k2, g_subln, w_attn_pw, w_out,
           g_ffn, w_ffn_gate, w_ffn_up, w_ffn_down, g_final):
    bsz, seq, d = x.shape
    assert g_mix.shape[0] == 1, "one layer only"
    l = 0
    lambda_init = 0.8 - 0.6 * math.exp(-0.3 * l)
    t = bsz * seq
    rope = _rope_tables(seq)
    u, q, k, v, gates = _in_proj(x, g_mix[l][None], w_in[l].astype(BF16),
                                 b_gate[l][None], rope)
    w_taps = jnp.broadcast_to(
        w_dw[l].astype(BF16).reshape(CONV_W, 1, d // LANES, LANES).transpose(2, 0, 1, 3),
        (d // LANES, CONV_W, PACKED_ROWS, LANES))
    cv = _conv_taps(u, w_taps, b_dw[l][None])
    lam_vecs = jnp.stack([lambda_q1[l], lambda_k1[l], lambda_q2[l], lambda_k2[l]])
    o = _attention(q, k, v, lam_vecs, g_subln[l][None], lambda_init)
    out = _out_ffn(x.reshape(t, d), o.reshape(t, d), cv.reshape(t, d),
                   gates.reshape(t, 2 * d), g_conv_ln[l][None], b_conv_ln[l][None],
                   w_conv_pw[l].astype(BF16), w_attn_pw[l].astype(BF16),
                   w_out[l].astype(BF16), g_ffn[l][None],
                   w_ffn_gate[l].astype(BF16), w_ffn_up[l].astype(BF16),
                   w_ffn_down[l].astype(BF16), g_final[None])
    return out.reshape(bsz, seq, d)
```
